```python
import math
import jax, jax.numpy as jnp
from jax import lax
import numpy as np

D_MODEL = 2048
BATCH = 16
SEQ = 256
DEPTH = 2
DEC_BATCH = 4
DEC_SEQ = 4096
PAST_LEN = 512

GRID_W = 64
N_EVEN = (DEPTH + 1) // 2
N_ODD = DEPTH // 2
H_A = 8
DH_A = 128
WIN_H = 8
WIN_W = 16
C_B = 1024
CONV_W = 31
H_C = 8
DH_C = 128
D_FF = 5632
N_SUB = 3
N_MOD = 3 * N_SUB
EPS = 1e-6
QBLOCK = 128
ROPE_BASE = 10000.0
W_A = H_A * DH_A
IN_A = 3 * W_A + 2 * C_B
OUT_A = W_A + C_B
IN_C = 2 * (2 * H_C * DH_C) + H_C * 2 * DH_C
OUT_C = H_C * 2 * DH_C

kernel_name = "hybrid_natten_conformer_diffattn_prefix_dit_step"


def _rms(x, g):
    xf = x.astype(jnp.float32)
    y = xf * lax.rsqrt(jnp.mean(xf * xf, axis=-1, keepdims=True) + EPS)
    return y.astype(x.dtype) * g


def _layernorm(x, g, b):
    xf = x.astype(jnp.float32)
    mu = jnp.mean(xf, axis=-1, keepdims=True)
    xc = xf - mu
    y = xc * lax.rsqrt(jnp.mean(xc * xc, axis=-1, keepdims=True) + EPS)
    return y.astype(x.dtype) * g + b


def _swiglu(h, w1, w3, w2):
    return (jax.nn.silu(h @ w1) * (h @ w3)) @ w2


def _adaln(x, m, i, g):
    return _rms(x, g) * (1.0 + m[:, None, 3 * i + 1]) + m[:, None, 3 * i]


def _gate(m, i):
    return m[:, None, 3 * i + 2]


def _heads(t, n_heads):
    b, n, _ = t.shape
    return t.reshape(b, n, n_heads, -1).transpose(0, 2, 1, 3)


def _merge(t):
    b, h, n, d = t.shape
    return t.transpose(0, 2, 1, 3).reshape(b, n, h * d)


def _query_blocks(fn, q):
    b, h, n, d = q.shape
    qb = q.reshape(b, h, n // QBLOCK, QBLOCK, d).transpose(2, 0, 1, 3, 4)
    out = lax.map(fn, qb)
    nb, _, ho, _, do = out.shape
    return out.transpose(1, 2, 0, 3, 4).reshape(b, ho, n, do)


def _attn_block(qb, k, v, scale):
    s = jnp.einsum('bhqd,bhkd->bhqk', qb, k).astype(jnp.float32) * scale
    p = jax.nn.softmax(s, axis=-1).astype(v.dtype)
    return jnp.einsum('bhqk,bhkd->bhqd', p, v)


def _diff_block(qb, k, v, lam, scale):
    b, h2, nq, _ = qb.shape
    s = jnp.einsum('bhqd,bhkd->bhqk', qb, k).astype(jnp.float32) * scale
    p = jax.nn.softmax(s, axis=-1).reshape(b, h2 // 2, 2, nq, -1)
    a = (p[:, :, 0] - lam * p[:, :, 1]).astype(v.dtype)
    return jnp.einsum('bhqk,bhkd->bhqd', a, v)


def _axial_rope(x):
    n, dh = x.shape[2], x.shape[3]
    t = jnp.arange(n)
    nf = dh // 4
    inv = ROPE_BASE ** (-jnp.arange(nf, dtype=jnp.float32) / nf)

    def rot(xh, pos):
        ang = pos.astype(jnp.float32)[:, None] * inv[None, :]
        cos = jnp.cos(ang).astype(x.dtype)
        sin = jnp.sin(ang).astype(x.dtype)
        x1, x2 = xh[..., :nf], xh[..., nf:]
        return jnp.concatenate([x1 * cos - x2 * sin, x2 * cos + x1 * sin], axis=-1)

    return jnp.concatenate([rot(x[..., : dh // 2], t // GRID_W), rot(x[..., dh // 2:], t % GRID_W)], axis=-1)


def _neighbourhood_attn(q, k, v, ck, cv, rpb):
    b, h, n, d = q.shape
    rows = n // GRID_W
    kh = min(WIN_H, rows)
    scale = d ** -0.5
    kg = k.reshape(b, h, rows, GRID_W, d)
    vg = v.reshape(b, h, rows, GRID_W, d)
    qg = q.reshape(b, h, rows, GRID_W, d).transpose(2, 0, 1, 3, 4)
    cols = jnp.arange(GRID_W)
    cs = jnp.clip(cols - WIN_W // 2, 0, GRID_W - WIN_W)
    col_mask = (cols[None, :] >= cs[:, None]) & (cols[None, :] < cs[:, None] + WIN_W)
    cidx = jnp.clip(cols[None, :] - cols[:, None] + WIN_W - 1, 0, 2 * WIN_W - 2)
    rpb_cols = rpb[:, :, cidx]

    def row_fn(args):
        r, qr = args
        rs = jnp.clip(r - kh // 2, 0, rows - kh)
        kb = lax.dynamic_slice_in_dim(kg, rs, kh, axis=2)
        vb = lax.dynamic_slice_in_dim(vg, rs, kh, axis=2)
        ridx = rs + jnp.arange(kh) - r + WIN_H - 1
        bias = rpb_cols[:, ridx].transpose(0, 2, 1, 3).astype(jnp.float32)
        s_lat = jnp.einsum('bhcd,bhjed->bhcje', qr, kb).astype(jnp.float32) * scale + bias[None]
        s_lat = jnp.where(col_mask[:, None, :], s_lat, -jnp.inf)
        s_ctx = jnp.einsum('bhcd,bhkd->bhck', qr, ck).astype(jnp.float32) * scale
        nc = s_ctx.shape[-1]
        s = jnp.concatenate([s_ctx, s_lat.reshape(b, h, GRID_W, kh * GRID_W)], axis=-1)
        p = jax.nn.softmax(s, axis=-1).astype(v.dtype)
        o_ctx = jnp.einsum('bhck,bhkd->bhcd', p[..., :nc], cv)
        o_lat = jnp.einsum('bhcje,bhjed->bhcd', p[..., nc:].reshape(b, h, GRID_W, kh, GRID_W), vb)
        return o_ctx + o_lat

    out = lax.map(row_fn, (jnp.arange(rows), qg))
    return out.transpose(1, 2, 0, 3, 4).reshape(b, h, n, d)


def _conv_module(u, dw_w, dw_b, ln_g, ln_b):
    a, gt = jnp.split(u, 2, axis=-1)
    z = a * jax.nn.sigmoid(gt)
    z = lax.conv_general_dilated(z, dw_w[:, None, :], window_strides=(1,),
                                 padding=((CONV_W // 2, CONV_W // 2),),
                                 dimension_numbers=('NWC', 'WIO', 'NWC'),
                                 feature_group_count=C_B) + dw_b
    return jax.nn.silu(_layernorm(z, ln_g, ln_b))


def _even_project(h, w_in):
    u = h @ w_in
    qa, ka, va, ub = jnp.split(u, [W_A, 2 * W_A, 3 * W_A], axis=-1)
    return _heads(qa, H_A), _heads(ka, H_A), _heads(va, H_A), ub


def _odd_project(h, w_in):
    u = h @ w_in
    q, k, v = jnp.split(u, [2 * H_C * DH_C, 4 * H_C * DH_C], axis=-1)
    return _heads(q, 2 * H_C), _heads(k, 2 * H_C), _heads(v, H_C)


def _diff_lambda(lp, lam_init):
    lf = lp.astype(jnp.float32)
    return jnp.exp(jnp.sum(lf[0] * lf[1])) - jnp.exp(jnp.sum(lf[2] * lf[3])) + lam_init


def _diff_out(o, subln_g, lam_init, w_out):
    return _merge(_rms(o, subln_g) * (1.0 - lam_init)) @ w_out


def setup_inputs(seed: int = 0) -> dict:
    key = jax.random.key(seed)
    ks = jax.random.split(key, 32)
    f32 = jnp.float32
    nrm = lambda k, shape, s: jax.random.normal(k, shape, f32) * s
    return {
        'x_prompt': nrm(ks[0], (BATCH, SEQ, D_MODEL), 1.0),
        'x_sample': nrm(ks[1], (DEC_BATCH, DEC_SEQ, D_MODEL), 1.0),
        'c': nrm(ks[2], (DEC_BATCH, D_MODEL), 1.0),
        'cache_a_k': nrm(ks[3], (DEC_BATCH, N_EVEN, H_A, PAST_LEN, DH_A), 1.0),
        'cache_a_v': nrm(ks[4], (DEC_BATCH, N_EVEN, H_A, PAST_LEN, DH_A), 1.0),
        'cache_c_k': nrm(ks[5], (DEC_BATCH, N_ODD, 2 * H_C, PAST_LEN, DH_C), 1.0),
        'cache_c_v': nrm(ks[6], (DEC_BATCH, N_ODD, H_C, PAST_LEN, 2 * DH_C), 1.0),
        'c_ctx': nrm(ks[7], (D_MODEL,), 1.0),
        'w_mod': nrm(ks[8], (DEPTH, D_MODEL, N_MOD * D_MODEL), 0.5 * D_MODEL ** -0.5),
        'b_mod': nrm(ks[9], (DEPTH, N_MOD * D_MODEL), 0.02),
        'norm_g': 1.0 + nrm(ks[10], (DEPTH, N_SUB, D_MODEL), 0.02),
        'ffn_w1': nrm(ks[11], (DEPTH, 2, D_MODEL, D_FF), D_MODEL ** -0.5),
        'ffn_w3': nrm(ks[12], (DEPTH, 2, D_MODEL, D_FF), D_MODEL ** -0.5),
        'ffn_w2': nrm(ks[13], (DEPTH, 2, D_FF, D_MODEL), D_FF ** -0.5),
        'a_w_in': nrm(ks[14], (N_EVEN, D_MODEL, IN_A), D_MODEL ** -0.5),
        'a_w_out': nrm(ks[15], (N_EVEN, OUT_A, D_MODEL), OUT_A ** -0.5),
        'a_rpb': nrm(ks[16], (N_EVEN, H_A, 2 * WIN_H - 1, 2 * WIN_W - 1), 0.1),
        'b_dw_w': nrm(ks[17], (N_EVEN, CONV_W, C_B), CONV_W ** -0.5),
        'b_dw_b': nrm(ks[18], (N_EVEN, C_B), 0.02),
        'b_ln_g': 1.0 + nrm(ks[19], (N_EVEN, C_B), 0.02),
        'b_ln_b': nrm(ks[20], (N_EVEN, C_B), 0.02),
        'c_w_in': nrm(ks[21], (N_ODD, D_MODEL, IN_C), D_MODEL ** -0.5),
        'c_w_out': nrm(ks[22], (N_ODD, OUT_C, D_MODEL), OUT_C ** -0.5),
        'c_lambda': nrm(ks[23], (N_ODD, 4, DH_C), 0.1),
        'c_subln_g': 1.0 + nrm(ks[24], (N_ODD, 2 * DH_C), 0.02),
        'final_g': 1.0 + nrm(ks[25], (D_MODEL,), 0.02),
    }


def reference(x_prompt, x_sample, c, cache_a_k, cache_a_v, cache_c_k, cache_c_v, c_ctx,
              w_mod, b_mod, norm_g, ffn_w1, ffn_w3, ffn_w2,
              a_w_in, a_w_out, a_rpb, b_dw_w, b_dw_b, b_ln_g, b_ln_b,
              c_w_in, c_w_out, c_lambda, c_subln_g, final_g):
    xp = x_prompt
    xs = x_sample
    new_a_k, new_a_v, new_c_k, new_c_v = [], [], [], []
    for l in range(DEPTH):
        m_ctx = (jax.nn.silu(c_ctx) @ w_mod[l] + b_mod[l]).reshape(1, N_MOD, D_MODEL)
        m_lat = (jax.nn.silu(c) @ w_mod[l] + b_mod[l]).reshape(-1, N_MOD, D_MODEL)
        g = norm_g[l]
        xp = xp + 0.5 * _gate(m_ctx, 0) * _swiglu(_adaln(xp, m_ctx, 0, g[0]), ffn_w1[l, 0], ffn_w3[l, 0], ffn_w2[l, 0])
        xs = xs + 0.5 * _gate(m_lat, 0) * _swiglu(_adaln(xs, m_lat, 0, g[0]), ffn_w1[l, 0], ffn_w3[l, 0], ffn_w2[l, 0])
        hp = _adaln(xp, m_ctx, 1, g[1])
        hs = _adaln(xs, m_lat, 1, g[1])
        if l % 2 == 0:
            e = l // 2
            conv_p = (b_dw_w[e], b_dw_b[e], b_ln_g[e], b_ln_b[e])
            q, k, v, ub = _even_project(hp, a_w_in[e])
            o_a = _query_blocks(lambda qb: _attn_block(qb, k, v, DH_A ** -0.5), q)
            yp = jnp.concatenate([_merge(o_a), _conv_module(ub, *conv_p)], axis=-1) @ a_w_out[e]
            new_a_k.append(k)
            new_a_v.append(v)
            q, k, v, ub = _even_project(hs, a_w_in[e])
            o_a = _neighbourhood_attn(q, k, v, cache_a_k[:, e], cache_a_v[:, e], a_rpb[e])
            ys = jnp.concatenate([_merge(o_a), _conv_module(ub, *conv_p)], axis=-1) @ a_w_out[e]
        else:
            o = l // 2
            lam_init = 0.8 - 0.6 * math.exp(-0.3 * l)
            lam = _diff_lambda(c_lambda[o], lam_init)
            sc = DH_C ** -0.5
            q, k, v = _odd_project(hp, c_w_in[o])
            op = _query_blocks(lambda qb: _diff_block(qb, k, v, lam, sc), q)
            yp = _diff_out(op, c_subln_g[o], lam_init, c_w_out[o])
            new_c_k.append(k)
            new_c_v.append(v)
            q, k, v = _odd_project(hs, c_w_in[o])
            q = _axial_rope(q)
            k_all = jnp.concatenate([cache_c_k[:, o], _axial_rope(k)], axis=2)
            v_all = jnp.concatenate([cache_c_v[:, o], v], axis=2)
            os_ = _query_blocks(lambda qb: _diff_block(qb, k_all, v_all, lam, sc), q)
            ys = _diff_out(os_, c_subln_g[o], lam_init, c_w_out[o])
        xp = xp + _gate(m_ctx, 1) * yp
        xs = xs + _gate(m_lat, 1) * ys
        xp = xp + 0.5 * _gate(m_ctx, 2) * _swiglu(_adaln(xp, m_ctx, 2, g[2]), ffn_w1[l, 1], ffn_w3[l, 1], ffn_w2[l, 1])
        xs = xs + 0.5 * _gate(m_lat, 2) * _swiglu(_adaln(xs, m_lat, 2, g[2]), ffn_w1[l, 1], ffn_w3[l, 1], ffn_w2[l, 1])
    y_prompt = _rms(xp, final_g)
    y_sample = _rms(xs, final_g)
    out_a_k = jnp.stack(new_a_k, axis=1)
    out_a_v = jnp.stack(new_a_v, axis=1)
    out_c_k = jnp.stack(new_c_k, axis=1)
    out_c_v = jnp.stack(new_c_v, axis=1)
    return (y_prompt, y_sample, out_a_k, out_a_v, out_c_k, out_c_v)
```

```python
import functools
import math

import jax
import jax.numpy as jnp
from jax import lax
from jax.experimental import pallas as pl
from jax.experimental.pallas import tpu as pltpu

F32 = jnp.float32
BF16 = jnp.bfloat16

GRID_W = 64
H_A = 8
DH_A = 128
WIN_H = 8
WIN_W = 16
CONV_W = 31
H_C = 8
DH_C = 128
N_SUB = 3
N_MOD = 3 * N_SUB
EPS = 1e-6
ROPE_BASE = 10000.0

LANES = 128
SUBLANES = 8
VMEM_CAP_BYTES = 64 * 1024 * 1024
NEG_BIG = -1e30

N_COND = 8
TM = 512
TF = 512
TN_PROJ = 1024
TN_MOD = 1024
CONV_TM = 128
CONV_HALO = 16
NA_QROWS = 8
NA_KROWS = 16
DIFF_TQ = 256


def _params(sem, vmem_bytes):
    limit = min(int(vmem_bytes * 1.25) + (4 << 20), VMEM_CAP_BYTES - (6 << 20))
    return pltpu.CompilerParams(dimension_semantics=sem, vmem_limit_bytes=limit)


def _sigmoid(x):
    return 1.0 / (1.0 + jnp.exp(-x))


def _dot(a, b):
    return jnp.dot(a, b, preferred_element_type=F32)


def _dot_nt(a, b):
    return lax.dot_general(a, b, (((1,), (1,)), ((), ())), preferred_element_type=F32)


def _adaln(x, m_ref, g_ref, sub):
    r = lax.rsqrt(jnp.mean(x * x, axis=-1, keepdims=True) + EPS)
    a = g_ref[...] * (1.0 + m_ref[3 * sub + 1:3 * sub + 2, :])
    return x * r * a + m_ref[3 * sub:3 * sub + 1, :]


def _mod_kernel(c_ref, w_ref, b_ref, o_ref):
    c = c_ref[...]
    s = (c * _sigmoid(c)).astype(BF16)
    o_ref[...] = _dot(s, w_ref[...].astype(BF16)) + b_ref[...]


def _modulation(conds, w_mod, b_mod):
    depth, d, n = w_mod.shape
    tn = TN_MOD
    vmem = 2 * d * tn * 4 + d * tn * 2 + 4 * N_COND * tn * 4
    out = pl.pallas_call(
        _mod_kernel,
        grid=(depth, n // tn),
        in_specs=[
            pl.BlockSpec((N_COND, d), lambda l, j: (0, 0)),
            pl.BlockSpec((None, d, tn), lambda l, j: (l, 0, j)),
            pl.BlockSpec((None, 1, tn), lambda l, j: (l, 0, j)),
        ],
        out_specs=pl.BlockSpec((None, N_COND, tn), lambda l, j: (l, 0, j)),
        out_shape=jax.ShapeDtypeStruct((depth, N_COND, n), F32),
        compiler_params=_params(("arbitrary", "arbitrary"), vmem),
        name="modulation",
    )(conds, w_mod, b_mod.reshape(depth, 1, n))
    return out.reshape(depth, N_COND, N_MOD, d)


def _ffn_kernel(x_ref, m_ref, g_ref, w1_ref, w3_ref, w2_ref, *rest, sub, final):
    if final:
        fg_ref, o_ref, h_scr, acc_scr = rest
    else:
        o_ref, h_scr, acc_scr = rest
    j = pl.program_id(1)

    @pl.when(j == 0)
    def _():
        h_scr[...] = _adaln(x_ref[...], m_ref, g_ref, sub).astype(BF16)
        acc_scr[...] = jnp.zeros_like(acc_scr)

    h = h_scr[...]
    a = _dot(h, w1_ref[...])
    b = _dot(h, w3_ref[...])
    t = (a * _sigmoid(a) * b).astype(BF16)
    acc_scr[...] += _dot(t, w2_ref[...])

    @pl.when(j == pl.num_programs(1) - 1)
    def _():
        y = x_ref[...] + (0.5 * m_ref[3 * sub + 2:3 * sub + 3, :]) * acc_scr[...]
        if final:
            y = y * lax.rsqrt(jnp.mean(y * y, axis=-1, keepdims=True) + EPS) * fg_ref[...]
        o_ref[...] = y


def _ffn(x, m, layer, cond_of_tile, g, w1, w3, w2, sub, final_g=None):
    t, d = x.shape
    f = w1.shape[1]
    final = final_g is not None
    in_specs = [
        pl.BlockSpec((TM, d), lambda i, j: (i, 0)),
        pl.BlockSpec((None, None, N_MOD, d), lambda i, j: (layer, cond_of_tile(i), 0, 0)),
        pl.BlockSpec((1, d), lambda i, j: (0, 0)),
        pl.BlockSpec((d, TF), lambda i, j: (0, j)),
        pl.BlockSpec((d, TF), lambda i, j: (0, j)),
        pl.BlockSpec((TF, d), lambda i, j: (j, 0)),
    ]
    args = [x, m, g.reshape(1, d), w1, w3, w2]
    if final:
        in_specs.append(pl.BlockSpec((1, d), lambda i, j: (0, 0)))
        args.append(final_g.reshape(1, d))
    vmem = (4 * TM * d * 4 + TM * d * 2 + TM * d * 4 + 2 * 3 * d * TF * 2
            + 3 * TM * TF * 4 + 2 * TM * d * 4)
    return pl.pallas_call(
        functools.partial(_ffn_kernel, sub=sub, final=final),
        grid=(t // TM, f // TF),
        in_specs=in_specs,
        out_specs=pl.BlockSpec((TM, d), lambda i, j: (i, 0)),
        out_shape=jax.ShapeDtypeStruct((t, d), F32),
        scratch_shapes=[pltpu.VMEM((TM, d), BF16), pltpu.VMEM((TM, d), F32)],
        compiler_params=_params(("parallel", "arbitrary"), vmem),
        name="ffn",
    )(*args)


def _rope(u, cos, sin, first_half):
    partner = jnp.where(first_half, pltpu.roll(u, 96, 1), pltpu.roll(u, 32, 1))
    return u * cos + partner * sin


def _proj_kernel(x_ref, m_ref, g_ref, w_ref, *rest, sub, n_rope_tiles):
    if n_rope_tiles:
        cos_ref, sin_ref, o_ref, h_scr = rest
    else:
        o_ref, h_scr = rest
    j = pl.program_id(1)

    @pl.when(j == 0)
    def _():
        h_scr[...] = _adaln(x_ref[...], m_ref, g_ref, sub).astype(BF16)

    u = _dot(h_scr[...], w_ref[...])
    if not n_rope_tiles:
        o_ref[...] = u.astype(o_ref.dtype)
        return

    @pl.when(j < n_rope_tiles)
    def _():
        cos = cos_ref[...]
        sin = sin_ref[...]
        lane = lax.broadcasted_iota(jnp.int32, cos.shape, 1)
        first_half = (lane % (DH_C // 2)) < (DH_C // 4)
        for hh in range(u.shape[1] // DH_C):
            sl = slice(hh * DH_C, (hh + 1) * DH_C)
            o_ref[:, sl] = _rope(u[:, sl], cos, sin, first_half).astype(o_ref.dtype)

    @pl.when(j >= n_rope_tiles)
    def _():
        o_ref[...] = u.astype(o_ref.dtype)


def _proj(x, m, layer, cond_of_tile, g, w, sub, out_dtype, rope=None, n_rope_cols=0):
    t, d = x.shape
    n = w.shape[1]
    tn = TN_PROJ
    in_specs = [
        pl.BlockSpec((TM, d), lambda i, j: (i, 0)),
        pl.BlockSpec((None, None, N_MOD, d), lambda i, j: (layer, cond_of_tile(i), 0, 0)),
        pl.BlockSpec((1, d), lambda i, j: (0, 0)),
        pl.BlockSpec((d, tn), lambda i, j: (0, j)),
    ]
    args = [x, m, g.reshape(1, d), w]
    if rope is not None:
        cos, sin = rope
        tiles_per_seq = cos.shape[0] // TM
        in_specs += [pl.BlockSpec((TM, DH_C), lambda i, j: (i % tiles_per_seq, 0))] * 2
        args += [cos, sin]
    osz = jnp.dtype(out_dtype).itemsize
    vmem = 2 * TM * d * 4 + TM * d * 2 + 2 * d * tn * 2 + 2 * TM * tn * osz + 2 * TM * tn * 4 + TM * d * 4
    return pl.pallas_call(
        functools.partial(_proj_kernel, sub=sub, n_rope_tiles=n_rope_cols // tn),
        grid=(t // TM, n // tn),
        in_specs=in_specs,
        out_specs=pl.BlockSpec((TM, tn), lambda i, j: (i, j)),
        out_shape=jax.ShapeDtypeStruct((t, n), out_dtype),
        scratch_shapes=[pltpu.VMEM((TM, d), BF16)],
        compiler_params=_params(("parallel", "arbitrary"), vmem),
        name="in_proj",
    )(*args)


def _rope_tables(n):
    t = jnp.arange(n)
    nf = DH_C // 4
    inv = ROPE_BASE ** (-jnp.arange(nf, dtype=F32) / nf)
    ang_r = (t // GRID_W).astype(F32)[:, None] * inv[None, :]
    ang_c = (t % GRID_W).astype(F32)[:, None] * inv[None, :]
    cos = jnp.concatenate([jnp.cos(ang_r)] * 2 + [jnp.cos(ang_c)] * 2, axis=-1)
    sin = jnp.concatenate([-jnp.sin(ang_r), jnp.sin(ang_r), -jnp.sin(ang_c), jnp.sin(ang_c)], axis=-1)
    return cos, sin


def _outproj_kernel(*refs, ks):
    ins = refs[:len(ks)]
    w_ref, x_ref, m_ref, o_ref = refs[len(ks):]
    y = None
    off = 0
    for r, k in zip(ins, ks):
        part = _dot(r[...], w_ref[off:off + k, :])
        y = part if y is None else y + part
        off += k
    o_ref[...] = x_ref[...] + m_ref[5:6, :] * y


def _outproj(ins, w, x, m, layer, cond_of_tile):
    t, d = x.shape
    ks = tuple(a.shape[1] for a in ins)
    kin = sum(ks)
    in_specs = [pl.BlockSpec((TM, k), lambda i: (i, 0)) for k in ks] + [
        pl.BlockSpec((kin, d), lambda i: (0, 0)),
        pl.BlockSpec((TM, d), lambda i: (i, 0)),
        pl.BlockSpec((None, None, N_MOD, d), lambda i: (layer, cond_of_tile(i), 0, 0)),
    ]
    vmem = 2 * kin * d * 2 + 2 * TM * kin * 2 + 4 * TM * d * 4 + 2 * TM * d * 4
    return pl.pallas_call(
        functools.partial(_outproj_kernel, ks=ks),
        grid=(t // TM,),
        in_specs=in_specs,
        out_specs=pl.BlockSpec((TM, d), lambda i: (i, 0)),
        out_shape=jax.ShapeDtypeStruct((t, d), F32),
        compiler_params=_params(("parallel",), vmem),
        name="out_proj",
    )(*ins, w, x, m)


def _attn_p_kernel(q_ref, k_ref, v_ref, o_ref, *, n_heads, dh):
    scale = dh ** -0.5
    for h in range(n_heads):
        sl = slice(h * dh, (h + 1) * dh)
        s = _dot_nt(q_ref[:, sl].astype(BF16), k_ref[:, sl].astype(BF16)) * scale
        e = jnp.exp(s - jnp.max(s, axis=-1, keepdims=True))
        r = 1.0 / jnp.sum(e, axis=-1, keepdims=True)
        o = _dot(e.astype(BF16), v_ref[:, sl].astype(BF16)) * r
        o_ref[:, sl] = o.astype(o_ref.dtype)


def _attn_prompt(u, n_batch, seq):
    w = H_A * DH_A
    vmem = 2 * 3 * seq * w * 4 + 2 * seq * w * 2 + 8 * seq * seq * 4
    return pl.pallas_call(
        functools.partial(_attn_p_kernel, n_heads=H_A, dh=DH_A),
        grid=(n_batch,),
        in_specs=[pl.BlockSpec((seq, w), lambda b: (b, 0)),
                  pl.BlockSpec((seq, w), lambda b: (b, 1)),
                  pl.BlockSpec((seq, w), lambda b: (b, 2))],
        out_specs=pl.BlockSpec((seq, w), lambda b: (b, 0)),
        out_shape=jax.ShapeDtypeStruct((n_batch * seq, w), BF16),
        compiler_params=_params(("parallel",), vmem),
        name="attn_prompt",
    )(u, u, u)


def _na_bias_table(rpb, rows):
    kh = min(WIN_H, rows)
    c = jnp.arange(GRID_W)
    cs = jnp.clip(c - WIN_W // 2, 0, GRID_W - WIN_W)
    col_ok = (c[None, :] >= cs[:, None]) & (c[None, :] < cs[:, None] + WIN_W)
    cidx = jnp.clip(c[None, :] - c[:, None] + WIN_W - 1, 0, 2 * WIN_W - 2)
    tables = []
    for r0 in (0, NA_QROWS, rows - NA_QROWS):
        base = min(max(r0 - (NA_KROWS - NA_QROWS) // 2, 0), rows - NA_KROWS)
        r = r0 + jnp.arange(NA_QROWS)
        kr = base + jnp.arange(NA_KROWS)
        rs = jnp.clip(r - kh // 2, 0, rows - kh)
        row_ok = (kr[None, :] >= rs[:, None]) & (kr[None, :] < rs[:, None] + kh)
        ridx = jnp.clip(kr[None, :] - r[:, None] + WIN_H - 1, 0, 2 * WIN_H - 2)
        bias = rpb[:, ridx[:, None, :, None], cidx[None, :, None, :]]
        ok = row_ok[:, None, :, None] & col_ok[None, :, None, :]
        tables.append(jnp.where(ok[None], bias.astype(F32), NEG_BIG).reshape(
            rpb.shape[0], NA_QROWS * GRID_W, NA_KROWS * GRID_W))
    return jnp.stack(tables)


def _na_kernel(q_ref, k_ref, v_ref, ck_ref, cv_ref, bias_ref, o_ref, *, rows):
    scale = DH_A ** -0.5
    j = pl.program_id(2)
    nk = NA_KROWS * GRID_W
    start = jnp.clip(j * NA_QROWS - (NA_KROWS - NA_QROWS) // 2, 0, rows - NA_KROWS) * GRID_W
    start = pl.multiple_of(start, 256)
    q = q_ref[...]
    s_c = _dot_nt(q, ck_ref[...].astype(BF16)) * scale
    s_l = _dot_nt(q, k_ref[pl.ds(start, nk), :]) * scale + bias_ref[...]
    mx = jnp.maximum(jnp.max(s_c, axis=-1, keepdims=True), jnp.max(s_l, axis=-1, keepdims=True))
    e_c = jnp.exp(s_c - mx)
    e_l = jnp.exp(s_l - mx)
    r = 1.0 / (jnp.sum(e_c, axis=-1, keepdims=True) + jnp.sum(e_l, axis=-1, keepdims=True))
    o = _dot(e_c.astype(BF16), cv_ref[...].astype(BF16)) + _dot(e_l.astype(BF16), v_ref[pl.ds(start, nk), :])
    o_ref[...] = (o * r).astype(o_ref.dtype)


def _na_latent(u, cache_k, cache_v, e, bias, n_batch, n):
    rows = n // GRID_W
    nj = rows // NA_QROWS
    tq = NA_QROWS * GRID_W
    nk = NA_KROWS * GRID_W
    ctx = cache_k.shape[3]
    w = H_A * DH_A

    def bias_idx(b, h, j):
        return ((j > 0).astype(jnp.int32) + (j == nj - 1).astype(jnp.int32), h, 0, 0)

    vmem = (2 * tq * DH_A * 2 * 2 + 2 * 2 * n * DH_A * 2 + 2 * 2 * ctx * DH_A * 4 + 2 * tq * nk * 4
            + 6 * tq * (nk + ctx) * 4)
    return pl.pallas_call(
        functools.partial(_na_kernel, rows=rows),
        grid=(n_batch, H_A, nj),
        in_specs=[
            pl.BlockSpec((tq, DH_A), lambda b, h, j: (b * nj + j, h)),
            pl.BlockSpec((n, DH_A), lambda b, h, j: (b, H_A + h)),
            pl.BlockSpec((n, DH_A), lambda b, h, j: (b, 2 * H_A + h)),
            pl.BlockSpec((None, None, None, ctx, DH_A), lambda b, h, j: (b, e, h, 0, 0)),
            pl.BlockSpec((None, None, None, ctx, DH_A), lambda b, h, j: (b, e, h, 0, 0)),
            pl.BlockSpec((None, None, tq, nk), bias_idx),
        ],
        out_specs=pl.BlockSpec((tq, DH_A), lambda b, h, j: (b * nj + j, h)),
        out_shape=jax.ShapeDtypeStruct((n_batch * n, w), BF16),
        compiler_params=_params(("parallel", "parallel", "arbitrary"), vmem),
        name="na_latent",
    )(u, u, u, cache_k, cache_v, bias)


def _conv_kernel(a_ref, g_ref, pa_ref, pg_ref, na_ref, ng_ref, w_ref, b_ref, lg_ref, lb_ref, o_ref,
                 z_scr, y_scr, *, tiles_per_seq):
    tm, c = o_ref.shape
    t = pl.program_id(0) % tiles_per_seq

    def glu(a, g):
        return a.astype(F32) * _sigmoid(g.astype(F32))

    z_scr[0:CONV_HALO, :] = jnp.where(t == 0, 0.0, glu(pa_ref[...], pg_ref[...]))
    z_scr[CONV_HALO:CONV_HALO + tm, :] = glu(a_ref[...], g_ref[...])
    z_scr[CONV_HALO + tm:, :] = jnp.where(t == tiles_per_seq - 1, 0.0, glu(na_ref[...], ng_ref[...]))

    first = CONV_HALO - CONV_W // 2
    for cb in range(c // LANES):
        cols = slice(cb * LANES, (cb + 1) * LANES)
        zc = z_scr[:, cols]
        acc = jnp.zeros((tm, LANES), F32)
        for b in range(SUBLANES):
            zb = zc[b:b + tm + 2 * CONV_HALO - SUBLANES]
            for a8 in range(2 * CONV_HALO // SUBLANES):
                k = SUBLANES * a8 + b - first
                if 0 <= k < CONV_W:
                    acc = acc + w_ref[k:k + 1, cols] * zb[SUBLANES * a8:SUBLANES * a8 + tm]
        y_scr[:, cols] = acc + b_ref[:, cols]

    y = y_scr[...]
    mu = jnp.mean(y, axis=-1, keepdims=True)
    yc = y - mu
    yn = yc * lax.rsqrt(jnp.mean(yc * yc, axis=-1, keepdims=True) + EPS) * lg_ref[...] + lb_ref[...]
    o_ref[...] = (yn * _sigmoid(yn)).astype(o_ref.dtype)


def _conv_module(u, col0, c, seq, dw_w, dw_b, ln_g, ln_b):
    t = u.shape[0]
    tm = CONV_TM
    tiles_per_seq = seq // tm
    hb = tm // CONV_HALO
    n_halo = t // CONV_HALO
    ca, cg = col0 // c, col0 // c + 1
    isz = u.dtype.itemsize
    in_specs = [
        pl.BlockSpec((tm, c), lambda i: (i, ca)),
        pl.BlockSpec((tm, c), lambda i: (i, cg)),
        pl.BlockSpec((CONV_HALO, c), lambda i: (jnp.maximum(i * hb - 1, 0), ca)),
        pl.BlockSpec((CONV_HALO, c), lambda i: (jnp.maximum(i * hb - 1, 0), cg)),
        pl.BlockSpec((CONV_HALO, c), lambda i: (jnp.minimum((i + 1) * hb, n_halo - 1), ca)),
        pl.BlockSpec((CONV_HALO, c), lambda i: (jnp.minimum((i + 1) * hb, n_halo - 1), cg)),
        pl.BlockSpec((CONV_W, c), lambda i: (0, 0)),
        pl.BlockSpec((1, c), lambda i: (0, 0)),
        pl.BlockSpec((1, c), lambda i: (0, 0)),
        pl.BlockSpec((1, c), lambda i: (0, 0)),
    ]
    vmem = (4 * (tm + 2 * CONV_HALO) * c * isz + 2 * tm * c * 2 + (2 * tm + 2 * CONV_HALO) * c * 4
            + 4 * CONV_W * c * 4 + 6 * tm * c * 4)
    return pl.pallas_call(
        functools.partial(_conv_kernel, tiles_per_seq=tiles_per_seq),
        grid=(t // tm,),
        in_specs=in_specs,
        out_specs=pl.BlockSpec((tm, c), lambda i: (i, 0)),
        out_shape=jax.ShapeDtypeStruct((t, c), BF16),
        scratch_shapes=[pltpu.VMEM((tm + 2 * CONV_HALO, c), F32), pltpu.VMEM((tm, c), F32)],
        compiler_params=_params(("parallel",), vmem),
        name="conv_module",
    )(u, u, u, u, u, u, dw_w, dw_b.reshape(1, c), ln_g.reshape(1, c), ln_b.reshape(1, c))


def _diff_lambda(lam_ref, lam_init):
    lp = lam_ref[...].astype(F32)
    s01 = jnp.sum(lp[0:1] * lp[1:2], axis=-1, keepdims=True)
    s23 = jnp.sum(lp[2:3] * lp[3:4], axis=-1, keepdims=True)
    return jnp.exp(s01) - jnp.exp(s23) + lam_init


def _subln(o, g_ref, lam_init):
    return o * lax.rsqrt(jnp.mean(o * o, axis=-1, keepdims=True) + EPS) * (g_ref[...] * (1.0 - lam_init))


def _diff_p_kernel(q_ref, k_ref, v_ref, lam_ref, g_ref, o_ref, *, lam_init):
    scale = DH_C ** -0.5
    lam = _diff_lambda(lam_ref, lam_init)
    dv = 2 * DH_C
    for h in range(H_C):
        p = []
        for i in range(2):
            sl = slice((2 * h + i) * DH_C, (2 * h + i + 1) * DH_C)
            s = _dot_nt(q_ref[:, sl].astype(BF16), k_ref[:, sl].astype(BF16)) * scale
            e = jnp.exp(s - jnp.max(s, axis=-1, keepdims=True))
            p.append(e * (1.0 / jnp.sum(e, axis=-1, keepdims=True)))
        a = (p[0] - lam * p[1]).astype(BF16)
        vs = slice(h * dv, (h + 1) * dv)
        o = _dot(a, v_ref[:, vs].astype(BF16))
        o_ref[:, vs] = _subln(o, g_ref, lam_init).astype(o_ref.dtype)


def _diff_prompt(u, lam_p, subln_g, lam_init, n_batch, seq):
    w = 2 * H_C * DH_C
    vmem = 2 * 3 * seq * w * 4 + 2 * seq * w * 2 + 10 * seq * seq * 4
    return pl.pallas_call(
        functools.partial(_diff_p_kernel, lam_init=lam_init),
        grid=(n_batch,),
        in_specs=[pl.BlockSpec((seq, w), lambda b: (b, 0)),
                  pl.BlockSpec((seq, w), lambda b: (b, 1)),
                  pl.BlockSpec((seq, w), lambda b: (b, 2)),
                  pl.BlockSpec((4, DH_C), lambda b: (0, 0)),
                  pl.BlockSpec((1, 2 * DH_C), lambda b: (0, 0))],
        out_specs=pl.BlockSpec((seq, w), lambda b: (b, 0)),
        out_shape=jax.ShapeDtypeStruct((n_batch * seq, w), BF16),
        compiler_params=_params(("parallel",), vmem),
        name="diff_prompt",
    )(u, u, u, lam_p, subln_g.reshape(1, 2 * DH_C))


def _diff_s_kernel(q_ref, k_ref, v_ref, ck_ref, cv_ref, lam_ref, g_ref, o_ref, *, lam_init):
    scale = DH_C ** -0.5
    lam = _diff_lambda(lam_ref, lam_init)
    pc, pk = [], []
    for i in range(2):
        sl = slice(i * DH_C, (i + 1) * DH_C)
        q = q_ref[:, sl]
        s_c = _dot_nt(q, ck_ref[i].astype(BF16)) * scale
        s_l = _dot_nt(q, k_ref[:, sl]) * scale
        mx = jnp.maximum(jnp.max(s_c, axis=-1, keepdims=True), jnp.max(s_l, axis=-1, keepdims=True))
        e_c = jnp.exp(s_c - mx)
        e_l = jnp.exp(s_l - mx)
        r = 1.0 / (jnp.sum(e_c, axis=-1, keepdims=True) + jnp.sum(e_l, axis=-1, keepdims=True))
        pc.append(e_c * r)
        pk.append(e_l * r)
    a_c = (pc[0] - lam * pc[1]).astype(BF16)
    a_l = (pk[0] - lam * pk[1]).astype(BF16)
    o = _dot(a_c, cv_ref[...].astype(BF16)) + _dot(a_l, v_ref[...])
    o_ref[...] = _subln(o, g_ref, lam_init).astype(o_ref.dtype)


def _diff_latent(u, cache_k, cache_v, o_idx, lam_p, subln_g, lam_init, n_batch, n):
    tq = DIFF_TQ
    nq = n // tq
    ctx = cache_k.shape[3]
    dv = 2 * DH_C
    w = H_C * dv
    nkh = w // dv
    vmem = (2 * tq * dv * 2 * 2 + 2 * 2 * n * dv * 2 + 2 * 2 * ctx * dv * 4
            + 8 * tq * (n + ctx) * 4)
    return pl.pallas_call(
        functools.partial(_diff_s_kernel, lam_init=lam_init),
        grid=(n_batch, H_C, nq),
        in_specs=[
            pl.BlockSpec((tq, dv), lambda b, h, i: (b * nq + i, h)),
            pl.BlockSpec((n, dv), lambda b, h, i: (b, nkh + h)),
            pl.BlockSpec((n, dv), lambda b, h, i: (b, 2 * nkh + h)),
            pl.BlockSpec((None, None, 2, ctx, DH_C), lambda b, h, i: (b, o_idx, h, 0, 0)),
            pl.BlockSpec((None, None, None, ctx, dv), lambda b, h, i: (b, o_idx, h, 0, 0)),
            pl.BlockSpec((4, DH_C), lambda b, h, i: (0, 0)),
            pl.BlockSpec((1, dv), lambda b, h, i: (0, 0)),
        ],
        out_specs=pl.BlockSpec((tq, dv), lambda b, h, i: (b * nq + i, h)),
        out_shape=jax.ShapeDtypeStruct((n_batch * n, w), BF16),
        compiler_params=_params(("parallel", "parallel", "arbitrary"), vmem),
        name="diff_latent",
    )(u, u, u, cache_k, cache_v, lam_p, subln_g.reshape(1, dv))


def _split_heads(u2d, n_batch, seq, col0, n_heads, dh):
    t = u2d[:, col0:col0 + n_heads * dh].reshape(n_batch, seq, n_heads, dh)
    return t.transpose(0, 2, 1, 3)


def kernel(x_prompt, x_sample, c, cache_a_k, cache_a_v, cache_c_k, cache_c_v, c_ctx, w_mod, b_mod, norm_g,
           ffn_w1, ffn_w3, ffn_w2, a_w_in, a_w_out, a_rpb, b_dw_w, b_dw_b, b_ln_g, b_ln_b, c_w_in, c_w_out,
           c_lambda, c_subln_g, final_g):
    bp, seq, d = x_prompt.shape
    bs, n, _ = x_sample.shape
    depth = w_mod.shape[0]
    assert 1 + bs <= N_COND and n % TM == 0 and (bp * seq) % TM == 0

    xp = x_prompt.reshape(bp * seq, d)
    xs = x_sample.reshape(bs * n, d)
    conds = jnp.concatenate([c_ctx[None], c, jnp.zeros((N_COND - 1 - bs, d), F32)], axis=0)
    m = _modulation(conds, w_mod, b_mod)

    tiles_per_latent = n // TM
    cond_p = lambda i: 0
    cond_s = lambda i: 1 + i // tiles_per_latent

    w1 = ffn_w1.astype(BF16)
    w3 = ffn_w3.astype(BF16)
    w2 = ffn_w2.astype(BF16)
    rope = _rope_tables(n)

    new_a_k, new_a_v, new_c_k, new_c_v = [], [], [], []
    for l in range(depth):
        g = norm_g[l]
        xp = _ffn(xp, m, l, cond_p, g[0], w1[l, 0], w3[l, 0], w2[l, 0], 0)
        xs = _ffn(xs, m, l, cond_s, g[0], w1[l, 0], w3[l, 0], w2[l, 0], 0)
        if l % 2 == 0:
            e = l // 2
            w_in = a_w_in[e].astype(BF16)
            w_out = a_w_out[e].astype(BF16)
            wa = H_A * DH_A
            cb = b_dw_w.shape[2]
            conv_p = (b_dw_w[e], b_dw_b[e], b_ln_g[e], b_ln_b[e])
            up = _proj(xp, m, l, cond_p, g[1], w_in, 1, F32)
            us = _proj(xs, m, l, cond_s, g[1], w_in, 1, BF16)
            new_a_k.append(_split_heads(up, bp, seq, wa, H_A, DH_A))
            new_a_v.append(_split_heads(up, bp, seq, 2 * wa, H_A, DH_A))
            op = _attn_prompt(up, bp, seq)
            cp = _conv_module(up, 3 * wa, cb, seq, *conv_p)
            bias = _na_bias_table(a_rpb[e], n // GRID_W)
            os_ = _na_latent(us, cache_a_k, cache_a_v, e, bias, bs, n)
            cs = _conv_module(us, 3 * wa, cb, n, *conv_p)
            xp = _outproj([op, cp], w_out, xp, m, l, cond_p)
            xs = _outproj([os_, cs], w_out, xs, m, l, cond_s)
        else:
            o = l // 2
            lam_init = 0.8 - 0.6 * math.exp(-0.3 * l)
            w_in = c_w_in[o].astype(BF16)
            w_out = c_w_out[o].astype(BF16)
            wqk = 2 * H_C * DH_C
            up = _proj(xp, m, l, cond_p, g[1], w_in, 1, F32)
            us = _proj(xs, m, l, cond_s, g[1], w_in, 1, BF16, rope=rope, n_rope_cols=2 * wqk)
            new_c_k.append(_split_heads(up, bp, seq, wqk, 2 * H_C, DH_C))
            new_c_v.append(_split_heads(up, bp, seq, 2 * wqk, H_C, 2 * DH_C))
            op = _diff_prompt(up, c_lambda[o], c_subln_g[o], lam_init, bp, seq)
            os_ = _diff_latent(us, cache_c_k, cache_c_v, o, c_lambda[o], c_subln_g[o], lam_init, bs, n)
            xp = _outproj([op], w_out, xp, m, l, cond_p)
            xs = _outproj([os_], w_out, xs, m, l, cond_s)
        fg = final_g if l == depth - 1 else None
        xp = _ffn(xp, m, l, cond_p, g[2], w1[l, 1], w3[l, 1], w2[l, 1], 2, final_g=fg)
        xs = _ffn(xs, m, l, cond_s, g[2], w1[l, 1], w3[l, 1], w2[l, 1], 2, final_g=fg)

    return (xp.reshape(bp, seq, d), xs.reshape(bs, n, d),
            jnp.stack(new_a_k, axis=1), jnp.stack(new_a_v, axis=1),
            jnp.stack(new_c_k, axis=1), jnp.stack(new_c_v, axis=1))
```

```python
import functools
import math

import jax
import jax.numpy as jnp
from jax import lax
from jax.experimental import pallas as pl
from jax.experimental.pallas import tpu as pltpu

F32 = jnp.float32
BF16 = jnp.bfloat16

GRID_W = 64
H_A = 8
DH_A = 128
WIN_H = 8
WIN_W = 16
CONV_W = 31
H_C = 8
DH_C = 128
N_SUB = 3
N_MOD = 3 * N_SUB
EPS = 1e-6
ROPE_BASE = 10000.0

LANES = 128
SUBLANES = 8
VMEM_CAP_BYTES = 64 * 1024 * 1024
NEG_BIG = -1e30

N_COND = 8
TM = 512
TF = 512
TN_PROJ = 1024
TN_MOD = 1024
CONV_TM = 128
CONV_HALO = 16
NA_QROWS = 8
NA_KROWS = 16
DIFF_TQ = 512
DIFF_KC = 512


def _params(sem, vmem_bytes):
    limit = min(int(vmem_bytes * 1.25) + (4 << 20), VMEM_CAP_BYTES - (6 << 20))
    return pltpu.CompilerParams(dimension_semantics=sem, vmem_limit_bytes=limit)


def _sigmoid(x):
    return 1.0 / (1.0 + jnp.exp(-x))


def _dot(a, b):
    return jnp.dot(a, b, preferred_element_type=F32)


def _dot_nt(a, b):
    return lax.dot_general(a, b, (((1,), (1,)), ((), ())), preferred_element_type=F32)


def _adaln(x, m_ref, g_ref, sub):
    r = lax.rsqrt(jnp.mean(x * x, axis=-1, keepdims=True) + EPS)
    a = g_ref[...] * (1.0 + m_ref[3 * sub + 1:3 * sub + 2, :])
    return x * r * a + m_ref[3 * sub:3 * sub + 1, :]


def _mod_kernel(c_ref, w_ref, b_ref, o_ref):
    c = c_ref[...]
    s = (c * _sigmoid(c)).astype(BF16)
    o_ref[...] = _dot(s, w_ref[...].astype(BF16)) + b_ref[...]


def _modulation(conds, w_mod, b_mod):
    depth, d, n = w_mod.shape
    tn = TN_MOD
    vmem = 2 * d * tn * 4 + d * tn * 2 + 4 * N_COND * tn * 4
    out = pl.pallas_call(
        _mod_kernel,
        grid=(depth, n // tn),
        in_specs=[
            pl.BlockSpec((N_COND, d), lambda l, j: (0, 0)),
            pl.BlockSpec((None, d, tn), lambda l, j: (l, 0, j)),
            pl.BlockSpec((None, 1, tn), lambda l, j: (l, 0, j)),
        ],
        out_specs=pl.BlockSpec((None, N_COND, tn), lambda l, j: (l, 0, j)),
        out_shape=jax.ShapeDtypeStruct((depth, N_COND, n), F32),
        compiler_params=_params(("arbitrary", "arbitrary"), vmem),
        name="modulation",
    )(conds, w_mod, b_mod.reshape(depth, 1, n))
    return out.reshape(depth, N_COND, N_MOD, d)


def _ffn_kernel(x_ref, m_ref, g_ref, w1_ref, w3_ref, w2_ref, *rest, sub, final):
    if final:
        fg_ref, o_ref, h_scr, acc_scr = rest
    else:
        o_ref, h_scr, acc_scr = rest
    j = pl.program_id(1)

    @pl.when(j == 0)
    def _():
        h_scr[...] = _adaln(x_ref[...], m_ref, g_ref, sub).astype(BF16)
        acc_scr[...] = jnp.zeros_like(acc_scr)

    h = h_scr[...]
    a = _dot(h, w1_ref[...])
    b = _dot(h, w3_ref[...])
    t = (a * _sigmoid(a) * b).astype(BF16)
    acc_scr[...] += _dot(t, w2_ref[...])

    @pl.when(j == pl.num_programs(1) - 1)
    def _():
        y = x_ref[...] + (0.5 * m_ref[3 * sub + 2:3 * sub + 3, :]) * acc_scr[...]
        if final:
            y = y * lax.rsqrt(jnp.mean(y * y, axis=-1, keepdims=True) + EPS) * fg_ref[...]
        o_ref[...] = y


def _ffn(x, m, layer, cond_of_tile, g, w1, w3, w2, sub, final_g=None):
    t, d = x.shape
    f = w1.shape[1]
    final = final_g is not None
    in_specs = [
        pl.BlockSpec((TM, d), lambda i, j: (i, 0)),
        pl.BlockSpec((None, None, N_MOD, d), lambda i, j: (layer, cond_of_tile(i), 0, 0)),
        pl.BlockSpec((1, d), lambda i, j: (0, 0)),
        pl.BlockSpec((d, TF), lambda i, j: (0, j)),
        pl.BlockSpec((d, TF), lambda i, j: (0, j)),
        pl.BlockSpec((TF, d), lambda i, j: (j, 0)),
    ]
    args = [x, m, g.reshape(1, d), w1, w3, w2]
    if final:
        in_specs.append(pl.BlockSpec((1, d), lambda i, j: (0, 0)))
        args.append(final_g.reshape(1, d))
    vmem = (4 * TM * d * 4 + TM * d * 2 + TM * d * 4 + 2 * 3 * d * TF * 2
            + 3 * TM * TF * 4 + 2 * TM * d * 4)
    return pl.pallas_call(
        functools.partial(_ffn_kernel, sub=sub, final=final),
        grid=(t // TM, f // TF),
        in_specs=in_specs,
        out_specs=pl.BlockSpec((TM, d), lambda i, j: (i, 0)),
        out_shape=jax.ShapeDtypeStruct((t, d), F32),
        scratch_shapes=[pltpu.VMEM((TM, d), BF16), pltpu.VMEM((TM, d), F32)],
        compiler_params=_params(("parallel", "arbitrary"), vmem),
        name="ffn",
    )(*args)


def _rope(u, cos, sin, first_half):
    partner = jnp.where(first_half, pltpu.roll(u, 96, 1), pltpu.roll(u, 32, 1))
    return u * cos + partner * sin


def _proj_kernel(x_ref, m_ref, g_ref, w_ref, *rest, sub, n_rope_tiles):
    if n_rope_tiles:
        cos_ref, sin_ref, o_ref, h_scr = rest
    else:
        o_ref, h_scr = rest
    j = pl.program_id(1)

    @pl.when(j == 0)
    def _():
        h_scr[...] = _adaln(x_ref[...], m_ref, g_ref, sub).astype(BF16)

    u = _dot(h_scr[...], w_ref[...])
    if not n_rope_tiles:
        o_ref[...] = u.astype(o_ref.dtype)
        return

    @pl.when(j < n_rope_tiles)
    def _():
        cos = cos_ref[...]
        sin = sin_ref[...]
        lane = lax.broadcasted_iota(jnp.int32, cos.shape, 1)
        first_half = (lane % (DH_C // 2)) < (DH_C // 4)
        for hh in range(u.shape[1] // DH_C):
            sl = slice(hh * DH_C, (hh + 1) * DH_C)
            o_ref[:, sl] = _rope(u[:, sl], cos, sin, first_half).astype(o_ref.dtype)

    @pl.when(j >= n_rope_tiles)
    def _():
        o_ref[...] = u.astype(o_ref.dtype)


def _proj(x, m, layer, cond_of_tile, g, w, sub, out_dtype, rope=None, n_rope_cols=0):
    t, d = x.shape
    n = w.shape[1]
    tn = TN_PROJ
    in_specs = [
        pl.BlockSpec((TM, d), lambda i, j: (i, 0)),
        pl.BlockSpec((None, None, N_MOD, d), lambda i, j: (layer, cond_of_tile(i), 0, 0)),
        pl.BlockSpec((1, d), lambda i, j: (0, 0)),
        pl.BlockSpec((d, tn), lambda i, j: (0, j)),
    ]
    args = [x, m, g.reshape(1, d), w]
    if rope is not None:
        cos, sin = rope
        tiles_per_seq = cos.shape[0] // TM
        in_specs += [pl.BlockSpec((TM, DH_C), lambda i, j: (i % tiles_per_seq, 0))] * 2
        args += [cos, sin]
    osz = jnp.dtype(out_dtype).itemsize
    vmem = 2 * TM * d * 4 + TM * d * 2 + 2 * d * tn * 2 + 2 * TM * tn * osz + 2 * TM * tn * 4 + TM * d * 4
    return pl.pallas_call(
        functools.partial(_proj_kernel, sub=sub, n_rope_tiles=n_rope_cols // tn),
        grid=(t // TM, n // tn),
        in_specs=in_specs,
        out_specs=pl.BlockSpec((TM, tn), lambda i, j: (i, j)),
        out_shape=jax.ShapeDtypeStruct((t, n), out_dtype),
        scratch_shapes=[pltpu.VMEM((TM, d), BF16)],
        compiler_params=_params(("parallel", "arbitrary"), vmem),
        name="in_proj",
    )(*args)


def _rope_tables(n):
    t = jnp.arange(n)
    nf = DH_C // 4
    inv = ROPE_BASE ** (-jnp.arange(nf, dtype=F32) / nf)
    ang_r = (t // GRID_W).astype(F32)[:, None] * inv[None, :]
    ang_c = (t % GRID_W).astype(F32)[:, None] * inv[None, :]
    cos = jnp.concatenate([jnp.cos(ang_r)] * 2 + [jnp.cos(ang_c)] * 2, axis=-1)
    sin = jnp.concatenate([-jnp.sin(ang_r), jnp.sin(ang_r), -jnp.sin(ang_c), jnp.sin(ang_c)], axis=-1)
    return cos, sin


def _outproj_kernel(*refs, ks):
    ins = refs[:len(ks)]
    w_ref, x_ref, m_ref, o_ref = refs[len(ks):]
    y = None
    off = 0
    for r, k in zip(ins, ks):
        part = _dot(r[...], w_ref[off:off + k, :])
        y = part if y is None else y + part
        off += k
    o_ref[...] = x_ref[...] + m_ref[5:6, :] * y


def _outproj(ins, w, x, m, layer, cond_of_tile):
    t, d = x.shape
    ks = tuple(a.shape[1] for a in ins)
    kin = sum(ks)
    in_specs = [pl.BlockSpec((TM, k), lambda i: (i, 0)) for k in ks] + [
        pl.BlockSpec((kin, d), lambda i: (0, 0)),
        pl.BlockSpec((TM, d), lambda i: (i, 0)),
        pl.BlockSpec((None, None, N_MOD, d), lambda i: (layer, cond_of_tile(i), 0, 0)),
    ]
    vmem = 2 * kin * d * 2 + 2 * TM * kin * 2 + 4 * TM * d * 4 + 2 * TM * d * 4
    return pl.pallas_call(
        functools.partial(_outproj_kernel, ks=ks),
        grid=(t // TM,),
        in_specs=in_specs,
        out_specs=pl.BlockSpec((TM, d), lambda i: (i, 0)),
        out_shape=jax.ShapeDtypeStruct((t, d), F32),
        compiler_params=_params(("parallel",), vmem),
        name="out_proj",
    )(*ins, w, x, m)


def _attn_p_kernel(q_ref, k_ref, v_ref, o_ref, *, n_heads, dh):
    scale = dh ** -0.5
    for h in range(n_heads):
        sl = slice(h * dh, (h + 1) * dh)
        s = _dot_nt(q_ref[:, sl].astype(BF16), k_ref[:, sl].astype(BF16)) * scale
        e = jnp.exp(s - jnp.max(s, axis=-1, keepdims=True))
        r = 1.0 / jnp.sum(e, axis=-1, keepdims=True)
        o = _dot(e.astype(BF16), v_ref[:, sl].astype(BF16)) * r
        o_ref[:, sl] = o.astype(o_ref.dtype)


def _attn_prompt(u, n_batch, seq):
    w = H_A * DH_A
    vmem = 2 * 3 * seq * w * 4 + 2 * seq * w * 2 + 8 * seq * seq * 4
    return pl.pallas_call(
        functools.partial(_attn_p_kernel, n_heads=H_A, dh=DH_A),
        grid=(n_batch,),
        in_specs=[pl.BlockSpec((seq, w), lambda b: (b, 0)),
                  pl.BlockSpec((seq, w), lambda b: (b, 1)),
                  pl.BlockSpec((seq, w), lambda b: (b, 2))],
        out_specs=pl.BlockSpec((seq, w), lambda b: (b, 0)),
        out_shape=jax.ShapeDtypeStruct((n_batch * seq, w), BF16),
        compiler_params=_params(("parallel",), vmem),
        name="attn_prompt",
    )(u, u, u)


def _na_bias_table(rpb, rows):
    h = rpb.shape[0]
    kh = min(WIN_H, rows)
    per_c = []
    for c in range(GRID_W):
        cs = min(max(c - WIN_W // 2, 0), GRID_W - WIN_W)
        lo = cs - c + WIN_W - 1
        per_c.append(jnp.pad(rpb[:, :, lo:lo + WIN_W], ((0, 0), (0, 0), (cs, GRID_W - WIN_W - cs)),
                             constant_values=NEG_BIG))
    tcol = jnp.stack(per_c, axis=2).astype(F32)
    tables = []
    for r0 in (0, NA_QROWS, rows - NA_QROWS):
        base = min(max(r0 - (NA_KROWS - NA_QROWS) // 2, 0), rows - NA_KROWS)
        per_r = []
        for r in range(r0, r0 + NA_QROWS):
            rs = min(max(r - kh // 2, 0), rows - kh)
            i0 = rs - r + WIN_H - 1
            slab = tcol[:, i0:i0 + kh].transpose(0, 2, 1, 3)
            per_r.append(jnp.pad(slab, ((0, 0), (0, 0), (rs - base, NA_KROWS - kh - (rs - base)), (0, 0)),
                                 constant_values=NEG_BIG))
        tables.append(jnp.stack(per_r, axis=1).reshape(h, NA_QROWS * GRID_W, NA_KROWS * GRID_W))
    return jnp.stack(tables)


def _na_kernel(q_ref, k_ref, v_ref, ck_ref, cv_ref, bias_ref, o_ref, *, rows):
    scale = DH_A ** -0.5
    j = pl.program_id(2)
    nk = NA_KROWS * GRID_W
    start = jnp.clip(j * NA_QROWS - (NA_KROWS - NA_QROWS) // 2, 0, rows - NA_KROWS) * GRID_W
    start = pl.multiple_of(start, 256)
    q = q_ref[...]
    s_c = _dot_nt(q, ck_ref[...].astype(BF16)) * scale
    s_l = _dot_nt(q, k_ref[pl.ds(start, nk), :]) * scale + bias_ref[...]
    mx = jnp.maximum(jnp.max(s_c, axis=-1, keepdims=True), jnp.max(s_l, axis=-1, keepdims=True))
    e_c = jnp.exp(s_c - mx)
    e_l = jnp.exp(s_l - mx)
    r = 1.0 / (jnp.sum(e_c, axis=-1, keepdims=True) + jnp.sum(e_l, axis=-1, keepdims=True))
    o = _dot(e_c.astype(BF16), cv_ref[...].astype(BF16)) + _dot(e_l.astype(BF16), v_ref[pl.ds(start, nk), :])
    o_ref[...] = (o * r).astype(o_ref.dtype)


def _na_latent(u, cache_k, cache_v, e, bias, n_batch, n):
    rows = n // GRID_W
    nj = rows // NA_QROWS
    tq = NA_QROWS * GRID_W
    nk = NA_KROWS * GRID_W
    ctx = cache_k.shape[3]
    w = H_A * DH_A

    def bias_idx(b, h, j):
        return ((j > 0).astype(jnp.int32) + (j == nj - 1).astype(jnp.int32), h, 0, 0)

    vmem = (2 * tq * DH_A * 2 * 2 + 2 * 2 * n * DH_A * 2 + 2 * 2 * ctx * DH_A * 4 + 2 * tq * nk * 4
            + 6 * tq * (nk + ctx) * 4)
    return pl.pallas_call(
        functools.partial(_na_kernel, rows=rows),
        grid=(n_batch, H_A, nj),
        in_specs=[
            pl.BlockSpec((tq, DH_A), lambda b, h, j: (b * nj + j, h)),
            pl.BlockSpec((n, DH_A), lambda b, h, j: (b, H_A + h)),
            pl.BlockSpec((n, DH_A), lambda b, h, j: (b, 2 * H_A + h)),
            pl.BlockSpec((None, None, None, ctx, DH_A), lambda b, h, j: (b, e, h, 0, 0)),
            pl.BlockSpec((None, None, None, ctx, DH_A), lambda b, h, j: (b, e, h, 0, 0)),
            pl.BlockSpec((None, None, tq, nk), bias_idx),
        ],
        out_specs=pl.BlockSpec((tq, DH_A), lambda b, h, j: (b * nj + j, h)),
        out_shape=jax.ShapeDtypeStruct((n_batch * n, w), BF16),
        compiler_params=_params(("parallel", "parallel", "arbitrary"), vmem),
        name="na_latent",
    )(u, u, u, cache_k, cache_v, bias)


def _conv_kernel(a_ref, g_ref, pa_ref, pg_ref, na_ref, ng_ref, w_ref, b_ref, lg_ref, lb_ref, o_ref,
                 z_scr, y_scr, *, tiles_per_seq):
    tm, c = o_ref.shape
    t = pl.program_id(0) % tiles_per_seq

    def glu(a, g):
        return a.astype(F32) * _sigmoid(g.astype(F32))

    z_scr[0:CONV_HALO, :] = jnp.where(t == 0, 0.0, glu(pa_ref[...], pg_ref[...]))
    z_scr[CONV_HALO:CONV_HALO + tm, :] = glu(a_ref[...], g_ref[...])
    z_scr[CONV_HALO + tm:, :] = jnp.where(t == tiles_per_seq - 1, 0.0, glu(na_ref[...], ng_ref[...]))

    first = CONV_HALO - CONV_W // 2
    for cb in range(c // LANES):
        cols = slice(cb * LANES, (cb + 1) * LANES)
        zc = z_scr[:, cols]
        acc = jnp.zeros((tm, LANES), F32)
        for b in range(SUBLANES):
            zb = zc[b:b + tm + 2 * CONV_HALO - SUBLANES]
            for a8 in range(2 * CONV_HALO // SUBLANES):
                k = SUBLANES * a8 + b - first
                if 0 <= k < CONV_W:
                    acc = acc + w_ref[k:k + 1, cols] * zb[SUBLANES * a8:SUBLANES * a8 + tm]
        y_scr[:, cols] = acc + b_ref[:, cols]

    y = y_scr[...]
    mu = jnp.mean(y, axis=-1, keepdims=True)
    yc = y - mu
    yn = yc * lax.rsqrt(jnp.mean(yc * yc, axis=-1, keepdims=True) + EPS) * lg_ref[...] + lb_ref[...]
    o_ref[...] = (yn * _sigmoid(yn)).astype(o_ref.dtype)


def _conv_module(u, col0, c, seq, dw_w, dw_b, ln_g, ln_b):
    t = u.shape[0]
    tm = CONV_TM
    tiles_per_seq = seq // tm
    hb = tm // CONV_HALO
    n_halo = t // CONV_HALO
    ca, cg = col0 // c, col0 // c + 1
    isz = u.dtype.itemsize
    in_specs = [
        pl.BlockSpec((tm, c), lambda i: (i, ca)),
        pl.BlockSpec((tm, c), lambda i: (i, cg)),
        pl.BlockSpec((CONV_HALO, c), lambda i: (jnp.maximum(i * hb - 1, 0), ca)),
        pl.BlockSpec((CONV_HALO, c), lambda i: (jnp.maximum(i * hb - 1, 0), cg)),
        pl.BlockSpec((CONV_HALO, c), lambda i: (jnp.minimum((i + 1) * hb, n_halo - 1), ca)),
        pl.BlockSpec((CONV_HALO, c), lambda i: (jnp.minimum((i + 1) * hb, n_halo - 1), cg)),
        pl.BlockSpec((CONV_W, c), lambda i: (0, 0)),
        pl.BlockSpec((1, c), lambda i: (0, 0)),
        pl.BlockSpec((1, c), lambda i: (0, 0)),
        pl.BlockSpec((1, c), lambda i: (0, 0)),
    ]
    vmem = (4 * (tm + 2 * CONV_HALO) * c * isz + 2 * tm * c * 2 + (2 * tm + 2 * CONV_HALO) * c * 4
            + 4 * CONV_W * c * 4 + 6 * tm * c * 4)
    return pl.pallas_call(
        functools.partial(_conv_kernel, tiles_per_seq=tiles_per_seq),
        grid=(t // tm,),
        in_specs=in_specs,
        out_specs=pl.BlockSpec((tm, c), lambda i: (i, 0)),
        out_shape=jax.ShapeDtypeStruct((t, c), BF16),
        scratch_shapes=[pltpu.VMEM((tm + 2 * CONV_HALO, c), F32), pltpu.VMEM((tm, c), F32)],
        compiler_params=_params(("parallel",), vmem),
        name="conv_module",
    )(u, u, u, u, u, u, dw_w, dw_b.reshape(1, c), ln_g.reshape(1, c), ln_b.reshape(1, c))


def _diff_lambda(lam_ref, lam_init):
    lp = lam_ref[...].astype(F32)
    s01 = jnp.sum(lp[0:1] * lp[1:2], axis=-1, keepdims=True)
    s23 = jnp.sum(lp[2:3] * lp[3:4], axis=-1, keepdims=True)
    return jnp.exp(s01) - jnp.exp(s23) + lam_init


def _subln(o, g_ref, lam_init):
    return o * lax.rsqrt(jnp.mean(o * o, axis=-1, keepdims=True) + EPS) * (g_ref[...] * (1.0 - lam_init))


def _diff_p_kernel(q_ref, k_ref, v_ref, lam_ref, g_ref, o_ref, *, lam_init):
    scale = DH_C ** -0.5
    lam = _diff_lambda(lam_ref, lam_init)
    dv = 2 * DH_C
    for h in range(H_C):
        p = []
        for i in range(2):
            sl = slice((2 * h + i) * DH_C, (2 * h + i + 1) * DH_C)
            s = _dot_nt(q_ref[:, sl].astype(BF16), k_ref[:, sl].astype(BF16)) * scale
            e = jnp.exp(s - jnp.max(s, axis=-1, keepdims=True))
            p.append(e * (1.0 / jnp.sum(e, axis=-1, keepdims=True)))
        a = (p[0] - lam * p[1]).astype(BF16)
        vs = slice(h * dv, (h + 1) * dv)
        o = _dot(a, v_ref[:, vs].astype(BF16))
        o_ref[:, vs] = _subln(o, g_ref, lam_init).astype(o_ref.dtype)


def _diff_prompt(u, lam_p, subln_g, lam_init, n_batch, seq):
    w = 2 * H_C * DH_C
    vmem = 2 * 3 * seq * w * 4 + 2 * seq * w * 2 + 10 * seq * seq * 4
    return pl.pallas_call(
        functools.partial(_diff_p_kernel, lam_init=lam_init),
        grid=(n_batch,),
        in_specs=[pl.BlockSpec((seq, w), lambda b: (b, 0)),
                  pl.BlockSpec((seq, w), lambda b: (b, 1)),
                  pl.BlockSpec((seq, w), lambda b: (b, 2)),
                  pl.BlockSpec((4, DH_C), lambda b: (0, 0)),
                  pl.BlockSpec((1, 2 * DH_C), lambda b: (0, 0))],
        out_specs=pl.BlockSpec((seq, w), lambda b: (b, 0)),
        out_shape=jax.ShapeDtypeStruct((n_batch * seq, w), BF16),
        compiler_params=_params(("parallel",), vmem),
        name="diff_prompt",
    )(u, u, u, lam_p, subln_g.reshape(1, 2 * DH_C))


def _diff_s_kernel(q_ref, k_ref, v_ref, ck_ref, cv_ref, lam_ref, g_ref, o_ref, ckb_scr, cvb_scr, *, lam_init):
    scale = DH_C ** -0.5

    @pl.when(pl.program_id(2) == 0)
    def _():
        ckb_scr[...] = ck_ref[...].astype(BF16)
        cvb_scr[...] = cv_ref[...].astype(BF16)

    n_chunks = k_ref.shape[0] // DIFF_KC
    outs = []
    for i in range(2):
        sl = slice(i * DH_C, (i + 1) * DH_C)
        q = q_ref[:, sl]
        mx = den = acc = None
        for c in range(n_chunks + 1):
            if c == 0:
                kc, vc = ckb_scr[i], cvb_scr[...]
            else:
                rows = slice((c - 1) * DIFF_KC, c * DIFF_KC)
                kc, vc = k_ref[rows, sl], v_ref[rows, :]
            s = _dot_nt(q, kc) * scale
            cmax = jnp.max(s, axis=-1, keepdims=True)
            if c == 0:
                mx = cmax
                e = jnp.exp(s - mx)
                den = jnp.sum(e, axis=-1, keepdims=True)
                acc = _dot(e.astype(BF16), vc)
            else:
                mx_new = jnp.maximum(mx, cmax)
                alpha = jnp.exp(mx - mx_new)
                e = jnp.exp(s - mx_new)
                den = den * alpha + jnp.sum(e, axis=-1, keepdims=True)
                acc = acc * alpha + _dot(e.astype(BF16), vc)
                mx = mx_new
        outs.append(acc * (1.0 / den))
    o = outs[0] - _diff_lambda(lam_ref, lam_init) * outs[1]
    o_ref[...] = _subln(o, g_ref, lam_init).astype(o_ref.dtype)


def _diff_latent(u, cache_k, cache_v, o_idx, lam_p, subln_g, lam_init, n_batch, n):
    tq = DIFF_TQ
    nq = n // tq
    ctx = cache_k.shape[3]
    dv = 2 * DH_C
    w = H_C * dv
    nkh = w // dv
    vmem = (2 * tq * dv * 2 * 2 + 2 * 2 * n * dv * 2 + 2 * 2 * ctx * dv * 4 + 2 * ctx * dv * 2
            + 8 * tq * DIFF_KC * 4 + 6 * tq * dv * 4)
    return pl.pallas_call(
        functools.partial(_diff_s_kernel, lam_init=lam_init),
        scratch_shapes=[pltpu.VMEM((2, ctx, DH_C), BF16), pltpu.VMEM((ctx, dv), BF16)],
        grid=(n_batch, H_C, nq),
        in_specs=[
            pl.BlockSpec((tq, dv), lambda b, h, i: (b * nq + i, h)),
            pl.BlockSpec((n, dv), lambda b, h, i: (b, nkh + h)),
            pl.BlockSpec((n, dv), lambda b, h, i: (b, 2 * nkh + h)),
            pl.BlockSpec((None, None, 2, ctx, DH_C), lambda b, h, i: (b, o_idx, h, 0, 0)),
            pl.BlockSpec((None, None, None, ctx, dv), lambda b, h, i: (b, o_idx, h, 0, 0)),
            pl.BlockSpec((4, DH_C), lambda b, h, i: (0, 0)),
            pl.BlockSpec((1, dv), lambda b, h, i: (0, 0)),
        ],
        out_specs=pl.BlockSpec((tq, dv), lambda b, h, i: (b * nq + i, h)),
        out_shape=jax.ShapeDtypeStruct((n_batch * n, w), BF16),
        compiler_params=_params(("parallel", "parallel", "arbitrary"), vmem),
        name="diff_latent",
    )(u, u, u, cache_k, cache_v, lam_p, subln_g.reshape(1, dv))


def _split_heads(u2d, n_batch, seq, col0, n_heads, dh):
    t = u2d[:, col0:col0 + n_heads * dh].reshape(n_batch, seq, n_heads, dh)
    return t.transpose(0, 2, 1, 3)


def kernel(x_prompt, x_sample, c, cache_a_k, cache_a_v, cache_c_k, cache_c_v, c_ctx, w_mod, b_mod, norm_g,
           ffn_w1, ffn_w3, ffn_w2, a_w_in, a_w_out, a_rpb, b_dw_w, b_dw_b, b_ln_g, b_ln_b, c_w_in, c_w_out,
           c_lambda, c_subln_g, final_g):
    bp, seq, d = x_prompt.shape
    bs, n, _ = x_sample.shape
    depth = w_mod.shape[0]
    assert 1 + bs <= N_COND and n % TM == 0 and (bp * seq) % TM == 0

    xp = x_prompt.reshape(bp * seq, d)
    xs = x_sample.reshape(bs * n, d)
    conds = jnp.concatenate([c_ctx[None], c, jnp.zeros((N_COND - 1 - bs, d), F32)], axis=0)
    m = _modulation(conds, w_mod, b_mod)

    tiles_per_latent = n // TM
    cond_p = lambda i: 0
    cond_s = lambda i: 1 + i // tiles_per_latent

    w1 = ffn_w1.astype(BF16)
    w3 = ffn_w3.astype(BF16)
    w2 = ffn_w2.astype(BF16)
    rope = _rope_tables(n)

    new_a_k, new_a_v, new_c_k, new_c_v = [], [], [], []
    for l in range(depth):
        g = norm_g[l]
        xp = _ffn(xp, m, l, cond_p, g[0], w1[l, 0], w3[l, 0], w2[l, 0], 0)
        xs = _ffn(xs, m, l, cond_s, g[0], w1[l, 0], w3[l, 0], w2[l, 0], 0)
        if l % 2 == 0:
            e = l // 2
            w_in = a_w_in[e].astype(BF16)
            w_out = a_w_out[e].astype(BF16)
            wa = H_A * DH_A
            cb = b_dw_w.shape[2]
            conv_p = (b_dw_w[e], b_dw_b[e], b_ln_g[e], b_ln_b[e])
            up = _proj(xp, m, l, cond_p, g[1], w_in, 1, F32)
            us = _proj(xs, m, l, cond_s, g[1], w_in, 1, BF16)
            new_a_k.append(_split_heads(up, bp, seq, wa, H_A, DH_A))
            new_a_v.append(_split_heads(up, bp, seq, 2 * wa, H_A, DH_A))
            op = _attn_prompt(up, bp, seq)
            cp = _conv_module(up, 3 * wa, cb, seq, *conv_p)
            bias = _na_bias_table(a_rpb[e], n // GRID_W)
            os_ = _na_latent(us, cache_a_k, cache_a_v, e, bias, bs, n)
            cs = _conv_module(us, 3 * wa, cb, n, *conv_p)
            xp = _outproj([op, cp], w_out, xp, m, l, cond_p)
            xs = _outproj([os_, cs], w_out, xs, m, l, cond_s)
        else:
            o = l // 2
            lam_init = 0.8 - 0.6 * math.exp(-0.3 * l)
            w_in = c_w_in[o].astype(BF16)
            w_out = c_w_out[o].astype(BF16)
            wqk = 2 * H_C * DH_C
            up = _proj(xp, m, l, cond_p, g[1], w_in, 1, F32)
            us = _proj(xs, m, l, cond_s, g[1], w_in, 1, BF16, rope=rope, n_rope_cols=2 * wqk)
            new_c_k.append(_split_heads(up, bp, seq, wqk, 2 * H_C, DH_C))
            new_c_v.append(_split_heads(up, bp, seq, 2 * wqk, H_C, 2 * DH_C))
            op = _diff_prompt(up, c_lambda[o], c_subln_g[o], lam_init, bp, seq)
            os_ = _diff_latent(us, cache_c_k, cache_c_v, o, c_lambda[o], c_subln_g[o], lam_init, bs, n)
            xp = _outproj([op], w_out, xp, m, l, cond_p)
            xs = _outproj([os_], w_out, xs, m, l, cond_s)
        fg = final_g if l == depth - 1 else None
        xp = _ffn(xp, m, l, cond_p, g[2], w1[l, 1], w3[l, 1], w2[l, 1], 2, final_g=fg)
        xs = _ffn(xs, m, l, cond_s, g[2], w1[l, 1], w3[l, 1], w2[l, 1], 2, final_g=fg)

    return (xp.reshape(bp, seq, d), xs.reshape(bs, n, d),
            jnp.stack(new_a_k, axis=1), jnp.stack(new_a_v, axis=1),
            jnp.stack(new_c_k, axis=1), jnp.stack(new_c_v, axis=1))
```

```python
import functools
import math

import jax
import jax.numpy as jnp
import numpy as np
from jax import lax
from jax.experimental import pallas as pl
from jax.experimental.pallas import tpu as pltpu

F32 = jnp.float32
BF16 = jnp.bfloat16

GRID_W = 64
H_A = 8
DH_A = 128
WIN_H = 8
WIN_W = 16
CONV_W = 31
H_C = 8
DH_C = 128
N_SUB = 3
N_MOD = 3 * N_SUB
EPS = 1e-6
ROPE_BASE = 10000.0
LOG2E = 1.0 / math.log(2.0)

LANES = 128
SUBLANES = 8
BF16_ROWS = 16
VMEM_CAP_BYTES = 64 * 1024 * 1024
NEG_BIG = -1e30

N_COND = 8
TM = 512
TF = 512
TN_PROJ = 1024
TN_MOD = 1024
CONV_TM = 128
CONV_HALO = 16
NA_QROWS = 8
NA_KROWS = 16
DIFF_TQ = 512
DIFF_KC = 1024
SM_RB = 32


def _params(sem, vmem_bytes):
    limit = min(int(vmem_bytes * 1.25) + (4 << 20), VMEM_CAP_BYTES - (6 << 20))
    return pltpu.CompilerParams(dimension_semantics=sem, vmem_limit_bytes=limit)


def _sigmoid(x):
    return 1.0 / (1.0 + jnp.exp(-x))


def _dot(a, b):
    return jnp.dot(a, b, preferred_element_type=F32)


def _dot_nt(a, b):
    return lax.dot_general(a, b, (((1,), (1,)), ((), ())), preferred_element_type=F32)


def _adaln(x, m_ref, g_ref, sub):
    r = lax.rsqrt(jnp.mean(x * x, axis=-1, keepdims=True) + EPS)
    a = g_ref[...] * (1.0 + m_ref[3 * sub + 1:3 * sub + 2, :])
    return x * r * a + m_ref[3 * sub:3 * sub + 1, :]


def _ahead_chunk(tm, n_steps):
    return pl.cdiv(pl.cdiv(tm, n_steps), BF16_ROWS) * BF16_ROWS


def _adaln_first(x_ref, m_ref, g_ref, h_scr, sub):
    i, j = pl.program_id(0), pl.program_id(1)

    @pl.when(jnp.logical_and(i == 0, j == 0))
    def _():
        h_scr[0] = _adaln(x_ref[...], m_ref, g_ref, sub).astype(BF16)

    return i % 2


def _adaln_ahead(xn_ref, mn_ref, g_ref, h_scr, slot, sub, chunk):
    j = pl.program_id(1)
    tm = xn_ref.shape[0]
    start = pl.multiple_of(jnp.minimum(j * chunk, tm - chunk), BF16_ROWS)
    rows = pl.ds(start, chunk)
    h_scr[1 - slot, rows, :] = _adaln(xn_ref[rows, :], mn_ref, g_ref, sub).astype(BF16)


def _tile_specs(d, n_tiles, layer, cond_of_tile, sub, first_only):
    nxt = lambda i: jnp.minimum(i + 1, n_tiles - 1)
    cur = (lambda i, j: (0, 0)) if first_only else (lambda i, j: (i, 0))
    mcur = ((lambda i, j: (layer, cond_of_tile(0), 0, 0)) if first_only
            else (lambda i, j: (layer, cond_of_tile(i), 0, 0)))
    return [
        pl.BlockSpec((TM, d), cur),
        pl.BlockSpec((TM, d), lambda i, j: (nxt(i), 0)),
        pl.BlockSpec((None, None, N_MOD, d), mcur),
        pl.BlockSpec((None, None, N_MOD, d), lambda i, j: (layer, cond_of_tile(nxt(i)), 0, 0)),
        pl.BlockSpec((None, None, 1, d), lambda i, j: (layer, sub, 0, 0)),
    ]


def _mod_kernel(c_ref, w_ref, b_ref, o_ref):
    c = c_ref[...]
    s = (c * _sigmoid(c)).astype(BF16)
    o_ref[...] = _dot(s, w_ref[...].astype(BF16)) + b_ref[...]


def _modulation(conds, w_mod, b_mod):
    depth, d, n = w_mod.shape
    tn = TN_MOD
    vmem = 2 * d * tn * 4 + d * tn * 2 + 4 * N_COND * tn * 4
    out = pl.pallas_call(
        _mod_kernel,
        grid=(depth, n // tn),
        in_specs=[
            pl.BlockSpec((N_COND, d), lambda l, j: (0, 0)),
            pl.BlockSpec((None, d, tn), lambda l, j: (l, 0, j)),
            pl.BlockSpec((None, 1, tn), lambda l, j: (l, 0, j)),
        ],
        out_specs=pl.BlockSpec((None, N_COND, tn), lambda l, j: (l, 0, j)),
        out_shape=jax.ShapeDtypeStruct((depth, N_COND, n), F32),
        compiler_params=_params(("arbitrary", "arbitrary"), vmem),
        name="modulation",
    )(conds, w_mod, b_mod.reshape(depth, 1, n))
    return out.reshape(depth, N_COND, N_MOD, d)


def _ffn_kernel(x_ref, xn_ref, m_ref, mn_ref, g_ref, w1_ref, w3_ref, w2_ref, *rest, sub, final, chunk):
    if final:
        fg_ref, o_ref, h_scr, acc_scr = rest
    else:
        o_ref, h_scr, acc_scr = rest
    i, j = pl.program_id(0), pl.program_id(1)

    @pl.when(jnp.logical_and(i == 0, j == 0))
    def _():
        acc_scr[...] = jnp.zeros_like(acc_scr)

    slot = _adaln_first(x_ref, m_ref, g_ref, h_scr, sub)
    h = h_scr[slot]
    a = _dot(h, w1_ref[...])
    b = _dot(h, w3_ref[...])
    _adaln_ahead(xn_ref, mn_ref, g_ref, h_scr, slot, sub, chunk)
    t = (a * _sigmoid(a) * b).astype(BF16)
    acc_scr[...] = jnp.where(j > 0, acc_scr[...], 0.0) + _dot(t, w2_ref[...])

    @pl.when(j == pl.num_programs(1) - 1)
    def _():
        y = x_ref[...] + (0.5 * m_ref[3 * sub + 2:3 * sub + 3, :]) * acc_scr[...]
        if final:
            y = y * lax.rsqrt(jnp.mean(y * y, axis=-1, keepdims=True) + EPS) * fg_ref[...]
        o_ref[...] = y


def _ffn(x, m, norm_g, layer, cond_of_tile, w1, w3, w2, which, sub, final_g=None):
    t, d = x.shape
    f = w1.shape[-1]
    nt, nj = t // TM, f // TF
    final = final_g is not None
    in_specs = _tile_specs(d, nt, layer, cond_of_tile, sub, first_only=False) + [
        pl.BlockSpec((None, None, d, TF), lambda i, j: (layer, which, 0, j)),
        pl.BlockSpec((None, None, d, TF), lambda i, j: (layer, which, 0, j)),
        pl.BlockSpec((None, None, TF, d), lambda i, j: (layer, which, j, 0)),
    ]
    args = [x, x, m, m, norm_g, w1, w3, w2]
    if final:
        in_specs.append(pl.BlockSpec((1, d), lambda i, j: (0, 0)))
        args.append(final_g.reshape(1, d))
    vmem = (6 * TM * d * 4 + 2 * TM * d * 2 + TM * d * 4 + 2 * 3 * d * TF * 2
            + 3 * TM * TF * 4 + TM * d * 4)
    return pl.pallas_call(
        functools.partial(_ffn_kernel, sub=sub, final=final, chunk=_ahead_chunk(TM, nj)),
        grid=(nt, nj),
        in_specs=in_specs,
        out_specs=pl.BlockSpec((TM, d), lambda i, j: (i, 0)),
        out_shape=jax.ShapeDtypeStruct((t, d), F32),
        scratch_shapes=[pltpu.VMEM((2, TM, d), BF16), pltpu.VMEM((TM, d), F32)],
        compiler_params=_params(("arbitrary", "arbitrary"), vmem),
        name="ffn",
    )(*args)


def _rope(u, cos, sin, first_half):
    partner = jnp.where(first_half, pltpu.roll(u, 96, 1), pltpu.roll(u, 32, 1))
    return u * cos + partner * sin


def _proj_kernel(x_ref, xn_ref, m_ref, mn_ref, g_ref, w_ref, *rest, sub, chunk, n_q_tiles, q_scale,
                 n_rope_tiles):
    if n_rope_tiles:
        cos_ref, sin_ref, o_ref, h_scr = rest
    else:
        o_ref, h_scr = rest
    j = pl.program_id(1)
    slot = _adaln_first(x_ref, m_ref, g_ref, h_scr, sub)
    u = _dot(h_scr[slot], w_ref[...]) * jnp.where(j < n_q_tiles, q_scale, 1.0)
    _adaln_ahead(xn_ref, mn_ref, g_ref, h_scr, slot, sub, chunk)
    if not n_rope_tiles:
        o_ref[...] = u.astype(o_ref.dtype)
        return
    roped = j < n_rope_tiles
    cos = jnp.where(roped, cos_ref[...], 1.0)
    sin = jnp.where(roped, sin_ref[...], 0.0)
    lane = lax.broadcasted_iota(jnp.int32, cos.shape, 1)
    first_half = (lane % (DH_C // 2)) < (DH_C // 4)
    for hh in range(u.shape[1] // DH_C):
        sl = slice(hh * DH_C, (hh + 1) * DH_C)
        o_ref[:, sl] = _rope(u[:, sl], cos, sin, first_half).astype(o_ref.dtype)


def _proj(x, m, norm_g, layer, cond_of_tile, w, widx, sub, out_dtype, n_q_cols, q_scale, rope=None,
          n_rope_cols=0):
    t, d = x.shape
    n = w.shape[-1]
    tn = TN_PROJ
    nt, nj = t // TM, n // tn
    assert n_q_cols % tn == 0 and n_rope_cols % tn == 0
    in_specs = _tile_specs(d, nt, layer, cond_of_tile, sub, first_only=True) + [
        pl.BlockSpec((None, d, tn), lambda i, j: (widx, 0, j)),
    ]
    args = [x, x, m, m, norm_g, w]
    if rope is not None:
        cos, sin = rope
        tiles_per_seq = cos.shape[0] // TM
        in_specs += [pl.BlockSpec((TM, DH_C), lambda i, j: (i % tiles_per_seq, 0))] * 2
        args += [cos, sin]
    osz = jnp.dtype(out_dtype).itemsize
    vmem = (4 * TM * d * 4 + 2 * TM * d * 2 + 2 * d * tn * 2 + 2 * TM * tn * osz + 3 * TM * tn * 4)
    return pl.pallas_call(
        functools.partial(_proj_kernel, sub=sub, chunk=_ahead_chunk(TM, nj), n_q_tiles=n_q_cols // tn,
                          q_scale=q_scale, n_rope_tiles=n_rope_cols // tn),
        grid=(nt, nj),
        in_specs=in_specs,
        out_specs=pl.BlockSpec((TM, tn), lambda i, j: (i, j)),
        out_shape=jax.ShapeDtypeStruct((t, n), out_dtype),
        scratch_shapes=[pltpu.VMEM((2, TM, d), BF16)],
        compiler_params=_params(("arbitrary", "arbitrary"), vmem),
        name="in_proj",
    )(*args)


def _rope_tables(n):
    t = jnp.arange(n)
    nf = DH_C // 4
    inv = ROPE_BASE ** (-jnp.arange(nf, dtype=F32) / nf)
    ang_r = (t // GRID_W).astype(F32)[:, None] * inv[None, :]
    ang_c = (t % GRID_W).astype(F32)[:, None] * inv[None, :]
    cos = jnp.concatenate([jnp.cos(ang_r)] * 2 + [jnp.cos(ang_c)] * 2, axis=-1)
    sin = jnp.concatenate([-jnp.sin(ang_r), jnp.sin(ang_r), -jnp.sin(ang_c), jnp.sin(ang_c)], axis=-1)
    return cos, sin


def _outproj_kernel(*refs, ks):
    ins = refs[:len(ks)]
    w_ref, x_ref, m_ref, o_ref = refs[len(ks):]
    y = None
    off = 0
    for r, k in zip(ins, ks):
        part = _dot(r[...], w_ref[off:off + k, :])
        y = part if y is None else y + part
        off += k
    o_ref[...] = x_ref[...] + m_ref[5:6, :] * y


def _outproj(ins, w, widx, x, m, layer, cond_of_tile):
    t, d = x.shape
    ks = tuple(a.shape[1] for a in ins)
    kin = sum(ks)
    in_specs = [pl.BlockSpec((TM, k), lambda i: (i, 0)) for k in ks] + [
        pl.BlockSpec((None, kin, d), lambda i: (widx, 0, 0)),
        pl.BlockSpec((TM, d), lambda i: (i, 0)),
        pl.BlockSpec((None, None, N_MOD, d), lambda i: (layer, cond_of_tile(i), 0, 0)),
    ]
    vmem = 2 * kin * d * 2 + 2 * TM * kin * 2 + 4 * TM * d * 4 + 2 * TM * d * 4
    return pl.pallas_call(
        functools.partial(_outproj_kernel, ks=ks),
        grid=(t // TM,),
        in_specs=in_specs,
        out_specs=pl.BlockSpec((TM, d), lambda i: (i, 0)),
        out_shape=jax.ShapeDtypeStruct((t, d), F32),
        compiler_params=_params(("parallel",), vmem),
        name="out_proj",
    )(*ins, w, x, m)


def _attn_p_kernel(q_ref, k_ref, v_ref, o_ref, *, n_heads, dh):
    for h in range(n_heads):
        sl = slice(h * dh, (h + 1) * dh)
        s = _dot_nt(q_ref[:, sl].astype(BF16), k_ref[:, sl].astype(BF16))
        e = jnp.exp2(s - jnp.max(s, axis=-1, keepdims=True))
        r = 1.0 / jnp.sum(e, axis=-1, keepdims=True)
        o = _dot(e.astype(BF16), v_ref[:, sl].astype(BF16)) * r
        o_ref[:, sl] = o.astype(o_ref.dtype)


def _attn_prompt(u, n_batch, seq):
    w = H_A * DH_A
    vmem = 2 * 3 * seq * w * 4 + 2 * seq * w * 2 + 8 * seq * seq * 4
    return pl.pallas_call(
        functools.partial(_attn_p_kernel, n_heads=H_A, dh=DH_A),
        grid=(n_batch,),
        in_specs=[pl.BlockSpec((seq, w), lambda b: (b, 0)),
                  pl.BlockSpec((seq, w), lambda b: (b, 1)),
                  pl.BlockSpec((seq, w), lambda b: (b, 2))],
        out_specs=pl.BlockSpec((seq, w), lambda b: (b, 0)),
        out_shape=jax.ShapeDtypeStruct((n_batch * seq, w), BF16),
        compiler_params=_params(("parallel",), vmem),
        name="attn_prompt",
    )(u, u, u)


def _na_bias_table(rpb, rows):
    h = rpb.shape[0]
    kh = min(WIN_H, rows)
    rpb2 = rpb.astype(F32) * LOG2E
    period = GRID_W + 2 * WIN_W - 1
    rp = jnp.pad(rpb2, ((0, 0), (0, 0), (0, GRID_W)), constant_values=NEG_BIG)
    skew = jnp.tile(rp, (1, 1, GRID_W))[:, :, :GRID_W * (period - 1)].reshape(h, -1, GRID_W, period - 1)
    skew = skew[:, :, :, WIN_W - 1:WIN_W - 1 + GRID_W]
    c = np.arange(GRID_W)
    cs = np.clip(c - WIN_W // 2, 0, GRID_W - WIN_W)
    col_ok = (c[None, :] >= cs[:, None]) & (c[None, :] < cs[:, None] + WIN_W)
    tcol = jnp.where(col_ok, skew, NEG_BIG)
    tables = []
    for r0 in (0, NA_QROWS, rows - NA_QROWS):
        base = min(max(r0 - (NA_KROWS - NA_QROWS) // 2, 0), rows - NA_KROWS)
        per_r = []
        for r in range(r0, r0 + NA_QROWS):
            rs = min(max(r - kh // 2, 0), rows - kh)
            i0 = rs - r + WIN_H - 1
            slab = tcol[:, i0:i0 + kh].transpose(0, 2, 1, 3)
            per_r.append(jnp.pad(slab, ((0, 0), (0, 0), (rs - base, NA_KROWS - kh - (rs - base)), (0, 0)),
                                 constant_values=NEG_BIG))
        tables.append(jnp.stack(per_r, axis=1).reshape(h, NA_QROWS * GRID_W, NA_KROWS * GRID_W))
    return jnp.stack(tables)


def _na_kernel(q_ref, k_ref, v_ref, ck_ref, cv_ref, bias_ref, o_ref, ckb_scr, cvb_scr, ec_scr, el_scr,
               r_scr, *, rows):
    j = pl.program_id(2)

    @pl.when(j == 0)
    def _():
        ckb_scr[...] = ck_ref[...].astype(BF16)
        cvb_scr[...] = cv_ref[...].astype(BF16)

    nk = NA_KROWS * GRID_W
    start = jnp.clip(j * NA_QROWS - (NA_KROWS - NA_QROWS) // 2, 0, rows - NA_KROWS) * GRID_W
    start = pl.multiple_of(start, 256)
    q = q_ref[...]
    s_c = _dot_nt(q, ckb_scr[...])
    s_l = _dot_nt(q, k_ref[pl.ds(start, nk), :])
    for rb in range(q.shape[0] // SM_RB):
        rws = slice(rb * SM_RB, (rb + 1) * SM_RB)
        sc = s_c[rws]
        sk = s_l[rws] + bias_ref[rws, :]
        mx = jnp.maximum(jnp.max(sc, axis=-1, keepdims=True), jnp.max(sk, axis=-1, keepdims=True))
        e_c = jnp.exp2(sc - mx)
        e_l = jnp.exp2(sk - mx)
        r_scr[rws, :] = 1.0 / (jnp.sum(e_c, axis=-1, keepdims=True) + jnp.sum(e_l, axis=-1, keepdims=True))
        ec_scr[rws, :] = e_c.astype(BF16)
        el_scr[rws, :] = e_l.astype(BF16)
    o = _dot(ec_scr[...], cvb_scr[...]) + _dot(el_scr[...], v_ref[pl.ds(start, nk), :])
    o_ref[...] = (o * r_scr[...]).astype(o_ref.dtype)


def _na_latent(u, cache_k, cache_v, e, bias, n_batch, n):
    rows = n // GRID_W
    nj = rows // NA_QROWS
    tq = NA_QROWS * GRID_W
    nk = NA_KROWS * GRID_W
    ctx = cache_k.shape[3]
    w = H_A * DH_A

    def bias_idx(b, h, j):
        return ((j > 0).astype(jnp.int32) + (j == nj - 1).astype(jnp.int32), h, 0, 0)

    vmem = (2 * tq * DH_A * 2 * 2 + 2 * 2 * n * DH_A * 2 + 2 * 2 * ctx * DH_A * 4 + 2 * tq * nk * 4
            + 2 * ctx * DH_A * 2 + tq * (nk + ctx) * 2 + 3 * tq * (nk + ctx) * 4)
    return pl.pallas_call(
        functools.partial(_na_kernel, rows=rows),
        grid=(n_batch, H_A, nj),
        in_specs=[
            pl.BlockSpec((tq, DH_A), lambda b, h, j: (b * nj + j, h)),
            pl.BlockSpec((n, DH_A), lambda b, h, j: (b, H_A + h)),
            pl.BlockSpec((n, DH_A), lambda b, h, j: (b, 2 * H_A + h)),
            pl.BlockSpec((None, None, None, ctx, DH_A), lambda b, h, j: (b, e, h, 0, 0)),
            pl.BlockSpec((None, None, None, ctx, DH_A), lambda b, h, j: (b, e, h, 0, 0)),
            pl.BlockSpec((None, None, tq, nk), bias_idx),
        ],
        out_specs=pl.BlockSpec((tq, DH_A), lambda b, h, j: (b * nj + j, h)),
        out_shape=jax.ShapeDtypeStruct((n_batch * n, w), BF16),
        scratch_shapes=[pltpu.VMEM((ctx, DH_A), BF16), pltpu.VMEM((ctx, DH_A), BF16),
                        pltpu.VMEM((tq, ctx), BF16), pltpu.VMEM((tq, nk), BF16),
                        pltpu.VMEM((tq, 1), F32)],
        compiler_params=_params(("parallel", "parallel", "arbitrary"), vmem),
        name="na_latent",
    )(u, u, u, cache_k, cache_v, bias)


def _conv_kernel(a_ref, g_ref, pa_ref, pg_ref, na_ref, ng_ref, w_ref, b_ref, lg_ref, lb_ref, o_ref,
                 z_scr, y_scr, *, tiles_per_seq):
    tm, c = o_ref.shape
    t = pl.program_id(0) % tiles_per_seq

    def glu(a, g):
        return a.astype(F32) * _sigmoid(g.astype(F32))

    z_scr[0:CONV_HALO, :] = jnp.where(t == 0, 0.0, glu(pa_ref[...], pg_ref[...]))
    z_scr[CONV_HALO:CONV_HALO + tm, :] = glu(a_ref[...], g_ref[...])
    z_scr[CONV_HALO + tm:, :] = jnp.where(t == tiles_per_seq - 1, 0.0, glu(na_ref[...], ng_ref[...]))

    first = CONV_HALO - CONV_W // 2
    for cb in range(c // LANES):
        cols = slice(cb * LANES, (cb + 1) * LANES)
        zc = z_scr[:, cols]
        acc = jnp.zeros((tm, LANES), F32)
        for b in range(SUBLANES):
            zb = zc[b:b + tm + 2 * CONV_HALO - SUBLANES]
            for a8 in range(2 * CONV_HALO // SUBLANES):
                k = SUBLANES * a8 + b - first
                if 0 <= k < CONV_W:
                    acc = acc + w_ref[k:k + 1, cols] * zb[SUBLANES * a8:SUBLANES * a8 + tm]
        y_scr[:, cols] = acc + b_ref[:, cols]

    y = y_scr[...]
    mu = jnp.mean(y, axis=-1, keepdims=True)
    yc = y - mu
    yn = yc * lax.rsqrt(jnp.mean(yc * yc, axis=-1, keepdims=True) + EPS) * lg_ref[...] + lb_ref[...]
    o_ref[...] = (yn * _sigmoid(yn)).astype(o_ref.dtype)


def _conv_module(u, col0, c, seq, dw_w, dw_b, ln_g, ln_b):
    t = u.shape[0]
    tm = CONV_TM
    tiles_per_seq = seq // tm
    hb = tm // CONV_HALO
    n_halo = t // CONV_HALO
    ca, cg = col0 // c, col0 // c + 1
    isz = u.dtype.itemsize
    in_specs = [
        pl.BlockSpec((tm, c), lambda i: (i, ca)),
        pl.BlockSpec((tm, c), lambda i: (i, cg)),
        pl.BlockSpec((CONV_HALO, c), lambda i: (jnp.maximum(i * hb - 1, 0), ca)),
        pl.BlockSpec((CONV_HALO, c), lambda i: (jnp.maximum(i * hb - 1, 0), cg)),
        pl.BlockSpec((CONV_HALO, c), lambda i: (jnp.minimum((i + 1) * hb, n_halo - 1), ca)),
        pl.BlockSpec((CONV_HALO, c), lambda i: (jnp.minimum((i + 1) * hb, n_halo - 1), cg)),
        pl.BlockSpec((CONV_W, c), lambda i: (0, 0)),
        pl.BlockSpec((1, c), lambda i: (0, 0)),
        pl.BlockSpec((1, c), lambda i: (0, 0)),
        pl.BlockSpec((1, c), lambda i: (0, 0)),
    ]
    vmem = (4 * (tm + 2 * CONV_HALO) * c * isz + 2 * tm * c * 2 + (2 * tm + 2 * CONV_HALO) * c * 4
            + 4 * CONV_W * c * 4 + 6 * tm * c * 4)
    return pl.pallas_call(
        functools.partial(_conv_kernel, tiles_per_seq=tiles_per_seq),
        grid=(t // tm,),
        in_specs=in_specs,
        out_specs=pl.BlockSpec((tm, c), lambda i: (i, 0)),
        out_shape=jax.ShapeDtypeStruct((t, c), BF16),
        scratch_shapes=[pltpu.VMEM((tm + 2 * CONV_HALO, c), F32), pltpu.VMEM((tm, c), F32)],
        compiler_params=_params(("parallel",), vmem),
        name="conv_module",
    )(u, u, u, u, u, u, dw_w, dw_b.reshape(1, c), ln_g.reshape(1, c), ln_b.reshape(1, c))


def _diff_lambda(lam_ref, lam_init):
    lp = lam_ref[...].astype(F32)
    s01 = jnp.sum(lp[0:1] * lp[1:2], axis=-1, keepdims=True)
    s23 = jnp.sum(lp[2:3] * lp[3:4], axis=-1, keepdims=True)
    return jnp.exp(s01) - jnp.exp(s23) + lam_init


def _subln(o, g_ref, lam_init):
    return o * lax.rsqrt(jnp.mean(o * o, axis=-1, keepdims=True) + EPS) * (g_ref[...] * (1.0 - lam_init))


def _diff_p_kernel(q_ref, k_ref, v_ref, lam_ref, g_ref, o_ref, *, lam_init):
    lam = _diff_lambda(lam_ref, lam_init)
    dv = 2 * DH_C
    for h in range(H_C):
        p = []
        for i in range(2):
            sl = slice((2 * h + i) * DH_C, (2 * h + i + 1) * DH_C)
            s = _dot_nt(q_ref[:, sl].astype(BF16), k_ref[:, sl].astype(BF16))
            e = jnp.exp2(s - jnp.max(s, axis=-1, keepdims=True))
            p.append(e * (1.0 / jnp.sum(e, axis=-1, keepdims=True)))
        a = (p[0] - lam * p[1]).astype(BF16)
        vs = slice(h * dv, (h + 1) * dv)
        o = _dot(a, v_ref[:, vs].astype(BF16))
        o_ref[:, vs] = _subln(o, g_ref, lam_init).astype(o_ref.dtype)


def _diff_prompt(u, lam_p, subln_g, lam_init, n_batch, seq):
    w = 2 * H_C * DH_C
    vmem = 2 * 3 * seq * w * 4 + 2 * seq * w * 2 + 10 * seq * seq * 4
    return pl.pallas_call(
        functools.partial(_diff_p_kernel, lam_init=lam_init),
        grid=(n_batch,),
        in_specs=[pl.BlockSpec((seq, w), lambda b: (b, 0)),
                  pl.BlockSpec((seq, w), lambda b: (b, 1)),
                  pl.BlockSpec((seq, w), lambda b: (b, 2)),
                  pl.BlockSpec((4, DH_C), lambda b: (0, 0)),
                  pl.BlockSpec((1, 2 * DH_C), lambda b: (0, 0))],
        out_specs=pl.BlockSpec((seq, w), lambda b: (b, 0)),
        out_shape=jax.ShapeDtypeStruct((n_batch * seq, w), BF16),
        compiler_params=_params(("parallel",), vmem),
        name="diff_prompt",
    )(u, u, u, lam_p, subln_g.reshape(1, 2 * DH_C))


def _diff_s_kernel(q_ref, k_ref, v_ref, ck_ref, cv_ref, lam_ref, g_ref, o_ref, ckb_scr, cvb_scr, e_scr,
                   acc_scr, *, lam_init):
    @pl.when(pl.program_id(2) == 0)
    def _():
        ckb_scr[...] = ck_ref[...].astype(BF16)
        cvb_scr[...] = cv_ref[...].astype(BF16)

    tq = q_ref.shape[0]
    n_chunks = k_ref.shape[0] // DIFF_KC
    n_rb = tq // SM_RB
    for i in range(2):
        sl = slice(i * DH_C, (i + 1) * DH_C)
        q = q_ref[:, sl]
        mx = [None] * n_rb
        den = [None] * n_rb
        for c in range(n_chunks + 1):
            if c == 0:
                kc, vc = ckb_scr[i], cvb_scr[...]
            else:
                krows = slice((c - 1) * DIFF_KC, c * DIFF_KC)
                kc, vc = k_ref[krows, sl], v_ref[krows, :]
            nkc = kc.shape[0]
            s = _dot_nt(q, kc)
            alpha = [None] * n_rb
            for rb in range(n_rb):
                rws = slice(rb * SM_RB, (rb + 1) * SM_RB)
                sb = s[rws]
                cmax = jnp.max(sb, axis=-1, keepdims=True)
                if c == 0:
                    mx[rb] = cmax
                    e = jnp.exp2(sb - cmax)
                    den[rb] = jnp.sum(e, axis=-1, keepdims=True)
                else:
                    m_new = jnp.maximum(mx[rb], cmax)
                    alpha[rb] = jnp.exp2(mx[rb] - m_new)
                    e = jnp.exp2(sb - m_new)
                    den[rb] = den[rb] * alpha[rb] + jnp.sum(e, axis=-1, keepdims=True)
                    mx[rb] = m_new
                e_scr[rws, :nkc] = e.astype(BF16)
            pv = _dot(e_scr[:, :nkc], vc)
            for rb in range(n_rb):
                rws = slice(rb * SM_RB, (rb + 1) * SM_RB)
                if c == 0:
                    acc_scr[i, rws, :] = pv[rws]
                else:
                    acc_scr[i, rws, :] = acc_scr[i, rws, :] * alpha[rb] + pv[rws]
        for rb in range(n_rb):
            rws = slice(rb * SM_RB, (rb + 1) * SM_RB)
            acc_scr[i, rws, :] = acc_scr[i, rws, :] * (1.0 / den[rb])
    o = acc_scr[0] - _diff_lambda(lam_ref, lam_init) * acc_scr[1]
    o_ref[...] = _subln(o, g_ref, lam_init).astype(o_ref.dtype)


def _diff_latent(u, cache_k, cache_v, o_idx, lam_p, subln_g, lam_init, n_batch, n):
    tq = DIFF_TQ
    nq = n // tq
    ctx = cache_k.shape[3]
    dv = 2 * DH_C
    w = H_C * dv
    nkh = w // dv
    kmax = max(DIFF_KC, ctx)
    vmem = (2 * tq * dv * 2 * 2 + 2 * 2 * n * dv * 2 + 2 * 2 * ctx * dv * 4 + 2 * ctx * dv * 2
            + tq * kmax * 2 + 2 * tq * dv * 4 + 3 * tq * kmax * 4 + 2 * tq * dv * 4)
    return pl.pallas_call(
        functools.partial(_diff_s_kernel, lam_init=lam_init),
        scratch_shapes=[pltpu.VMEM((2, ctx, DH_C), BF16), pltpu.VMEM((ctx, dv), BF16),
                        pltpu.VMEM((tq, kmax), BF16), pltpu.VMEM((2, tq, dv), F32)],
        grid=(n_batch, H_C, nq),
        in_specs=[
            pl.BlockSpec((tq, dv), lambda b, h, i: (b * nq + i, h)),
            pl.BlockSpec((n, dv), lambda b, h, i: (b, nkh + h)),
            pl.BlockSpec((n, dv), lambda b, h, i: (b, 2 * nkh + h)),
            pl.BlockSpec((None, None, 2, ctx, DH_C), lambda b, h, i: (b, o_idx, h, 0, 0)),
            pl.BlockSpec((None, None, None, ctx, dv), lambda b, h, i: (b, o_idx, h, 0, 0)),
            pl.BlockSpec((4, DH_C), lambda b, h, i: (0, 0)),
            pl.BlockSpec((1, dv), lambda b, h, i: (0, 0)),
        ],
        out_specs=pl.BlockSpec((tq, dv), lambda b, h, i: (b * nq + i, h)),
        out_shape=jax.ShapeDtypeStruct((n_batch * n, w), BF16),
        compiler_params=_params(("parallel", "parallel", "arbitrary"), vmem),
        name="diff_latent",
    )(u, u, u, cache_k, cache_v, lam_p, subln_g.reshape(1, dv))


def _split_heads(u2d, n_batch, seq, col0, n_heads, dh):
    t = u2d[:, col0:col0 + n_heads * dh].reshape(n_batch, seq, n_heads, dh)
    return t.transpose(0, 2, 1, 3)


def kernel(x_prompt, x_sample, c, cache_a_k, cache_a_v, cache_c_k, cache_c_v, c_ctx, w_mod, b_mod, norm_g,
           ffn_w1, ffn_w3, ffn_w2, a_w_in, a_w_out, a_rpb, b_dw_w, b_dw_b, b_ln_g, b_ln_b, c_w_in, c_w_out,
           c_lambda, c_subln_g, final_g):
    bp, seq, d = x_prompt.shape
    bs, n, _ = x_sample.shape
    depth = w_mod.shape[0]
    assert 1 + bs <= N_COND and n % TM == 0 and (bp * seq) % TM == 0

    xp = x_prompt.reshape(bp * seq, d)
    xs = x_sample.reshape(bs * n, d)
    conds = jnp.concatenate([c_ctx[None], c, jnp.zeros((N_COND - 1 - bs, d), F32)], axis=0)
    m = _modulation(conds, w_mod, b_mod)

    tiles_per_latent = n // TM
    cond_p = lambda i: 0
    cond_s = lambda i: 1 + i // tiles_per_latent

    w1 = ffn_w1.astype(BF16)
    w3 = ffn_w3.astype(BF16)
    w2 = ffn_w2.astype(BF16)
    a_in, a_out = a_w_in.astype(BF16), a_w_out.astype(BF16)
    c_in, c_out = c_w_in.astype(BF16), c_w_out.astype(BF16)
    ng = norm_g.reshape(depth, N_SUB, 1, d)
    rope = _rope_tables(n)

    new_a_k, new_a_v, new_c_k, new_c_v = [], [], [], []
    for l in range(depth):
        xp = _ffn(xp, m, ng, l, cond_p, w1, w3, w2, 0, 0)
        xs = _ffn(xs, m, ng, l, cond_s, w1, w3, w2, 0, 0)
        if l % 2 == 0:
            e = l // 2
            wa = H_A * DH_A
            cb = b_dw_w.shape[2]
            conv_p = (b_dw_w[e], b_dw_b[e], b_ln_g[e], b_ln_b[e])
            qs = DH_A ** -0.5 * LOG2E
            up = _proj(xp, m, ng, l, cond_p, a_in, e, 1, F32, wa, qs)
            us = _proj(xs, m, ng, l, cond_s, a_in, e, 1, BF16, wa, qs)
            new_a_k.append(_split_heads(up, bp, seq, wa, H_A, DH_A))
            new_a_v.append(_split_heads(up, bp, seq, 2 * wa, H_A, DH_A))
            op = _attn_prompt(up, bp, seq)
            cp = _conv_module(up, 3 * wa, cb, seq, *conv_p)
            bias = _na_bias_table(a_rpb[e], n // GRID_W)
            os_ = _na_latent(us, cache_a_k, cache_a_v, e, bias, bs, n)
            cs = _conv_module(us, 3 * wa, cb, n, *conv_p)
            xp = _outproj([op, cp], a_out, e, xp, m, l, cond_p)
            xs = _outproj([os_, cs], a_out, e, xs, m, l, cond_s)
        else:
            o = l // 2
            lam_init = 0.8 - 0.6 * math.exp(-0.3 * l)
            wqk = 2 * H_C * DH_C
            qs = DH_C ** -0.5 * LOG2E
            up = _proj(xp, m, ng, l, cond_p, c_in, o, 1, F32, wqk, qs)
            us = _proj(xs, m, ng, l, cond_s, c_in, o, 1, BF16, wqk, qs, rope=rope, n_rope_cols=2 * wqk)
            new_c_k.append(_split_heads(up, bp, seq, wqk, 2 * H_C, DH_C))
            new_c_v.append(_split_heads(up, bp, seq, 2 * wqk, H_C, 2 * DH_C))
            op = _diff_prompt(up, c_lambda[o], c_subln_g[o], lam_init, bp, seq)
            os_ = _diff_latent(us, cache_c_k, cache_c_v, o, c_lambda[o], c_subln_g[o], lam_init, bs, n)
            xp = _outproj([op], c_out, o, xp, m, l, cond_p)
            xs = _outproj([os_], c_out, o, xs, m, l, cond_s)
        fg = final_g if l == depth - 1 else None
        xp = _ffn(xp, m, ng, l, cond_p, w1, w3, w2, 1, 2, final_g=fg)
        xs = _ffn(xs, m, ng, l, cond_s, w1, w3, w2, 1, 2, final_g=fg)

    return (xp.reshape(bp, seq, d), xs.reshape(bs, n, d),
            jnp.stack(new_a_k, axis=1), jnp.stack(new_a_v, axis=1),
            jnp.stack(new_c_k, axis=1), jnp.stack(new_c_v, axis=1))
```

```python
import functools
import math

import jax
import jax.numpy as jnp
import numpy as np
from jax import lax
from jax.experimental import pallas as pl
from jax.experimental.pallas import tpu as pltpu

F32 = jnp.float32
BF16 = jnp.bfloat16

GRID_W = 64
H_A = 8
DH_A = 128
WIN_H = 8
WIN_W = 16
CONV_W = 31
H_C = 8
DH_C = 128
N_SUB = 3
N_MOD = 3 * N_SUB
EPS = 1e-6
ROPE_BASE = 10000.0
LOG2E = 1.0 / math.log(2.0)

LANES = 128
SUBLANES = 8
BF16_ROWS = 16
VMEM_CAP_BYTES = 64 * 1024 * 1024
NEG_BIG = -1e30

N_COND = 8
TM = 512
TM_FFN = 1024
TM_F32_OUT = 256
TF = 512
TN_PROJ = 1024
TN_MOD = 1024
CONV_TM = 128
CONV_HALO = 16
NA_QROWS = 8
NA_KROWS = 16
DIFF_TQ = 512
DIFF_KC = 1024
SM_RB = 32


def _params(sem, vmem_bytes):
    limit = min(int(vmem_bytes * 1.25) + (4 << 20), VMEM_CAP_BYTES - (6 << 20))
    return pltpu.CompilerParams(dimension_semantics=sem, vmem_limit_bytes=limit)


def _sigmoid(x):
    return 1.0 / (1.0 + jnp.exp(-x))


def _dot(a, b):
    return jnp.dot(a, b, preferred_element_type=F32)


def _dot_nt(a, b):
    return lax.dot_general(a, b, (((1,), (1,)), ((), ())), preferred_element_type=F32)


def _adaln(x, m_ref, g_ref, sub):
    r = lax.rsqrt(jnp.mean(x * x, axis=-1, keepdims=True) + EPS)
    a = g_ref[...] * (1.0 + m_ref[3 * sub + 1:3 * sub + 2, :])
    return x * r * a + m_ref[3 * sub:3 * sub + 1, :]


def _row_splits(tm, n_parts):
    cuts = [round(k * tm / n_parts / BF16_ROWS) * BF16_ROWS for k in range(n_parts + 1)]
    return list(zip(cuts[:-1], cuts[1:]))


def _mod_kernel(c_ref, w_ref, b_ref, o_ref):
    c = c_ref[...]
    s = (c * _sigmoid(c)).astype(BF16)
    o_ref[...] = _dot(s, w_ref[...].astype(BF16)) + b_ref[...]


def _modulation(conds, w_mod, b_mod):
    depth, d, n = w_mod.shape
    tn = TN_MOD
    vmem = 2 * d * tn * 4 + d * tn * 2 + 4 * N_COND * tn * 4
    out = pl.pallas_call(
        _mod_kernel,
        grid=(depth, n // tn),
        in_specs=[
            pl.BlockSpec((N_COND, d), lambda l, j: (0, 0)),
            pl.BlockSpec((None, d, tn), lambda l, j: (l, 0, j)),
            pl.BlockSpec((None, 1, tn), lambda l, j: (l, 0, j)),
        ],
        out_specs=pl.BlockSpec((None, N_COND, tn), lambda l, j: (l, 0, j)),
        out_shape=jax.ShapeDtypeStruct((depth, N_COND, n), F32),
        compiler_params=_params(("arbitrary", "arbitrary"), vmem),
        name="modulation",
    )(conds, w_mod, b_mod.reshape(depth, 1, n))
    return out.reshape(depth, N_COND, N_MOD, d)


def _ffn_kernel(x_ref, m_ref, g_ref, w1_ref, w3_ref, w2_ref, *rest, sub, final):
    if final:
        fg_ref, o_ref, h_scr = rest
    else:
        o_ref, h_scr = rest
    j = pl.program_id(1)

    def dff_tile():
        h = h_scr[...]
        a = _dot(h, w1_ref[...])
        b = _dot(h, w3_ref[...])
        return _dot((a * _sigmoid(a) * b).astype(BF16), w2_ref[...])

    @pl.when(j == 0)
    def _():
        h_scr[...] = _adaln(x_ref[...], m_ref, g_ref, sub).astype(BF16)
        o_ref[...] = dff_tile()

    @pl.when(j > 0)
    def _():
        o_ref[...] += dff_tile()

    @pl.when(j == pl.num_programs(1) - 1)
    def _():
        y = x_ref[...] + (0.5 * m_ref[3 * sub + 2:3 * sub + 3, :]) * o_ref[...]
        if final:
            y = y * lax.rsqrt(jnp.mean(y * y, axis=-1, keepdims=True) + EPS) * fg_ref[...]
        o_ref[...] = y


def _ffn(x, m, norm_g, layer, cond_of_tile, w1, w3, w2, which, sub, final_g=None):
    t, d = x.shape
    f = w1.shape[-1]
    tm = TM_FFN
    final = final_g is not None
    in_specs = [
        pl.BlockSpec((tm, d), lambda i, j: (i, 0)),
        pl.BlockSpec((None, None, N_MOD, d), lambda i, j: (layer, cond_of_tile(i * (tm // TM)), 0, 0)),
        pl.BlockSpec((None, None, 1, d), lambda i, j: (layer, sub, 0, 0)),
        pl.BlockSpec((None, None, d, TF), lambda i, j: (layer, which, 0, j)),
        pl.BlockSpec((None, None, d, TF), lambda i, j: (layer, which, 0, j)),
        pl.BlockSpec((None, None, TF, d), lambda i, j: (layer, which, j, 0)),
    ]
    args = [x, m, norm_g, w1, w3, w2]
    if final:
        in_specs.append(pl.BlockSpec((1, d), lambda i, j: (0, 0)))
        args.append(final_g.reshape(1, d))
    vmem = 4 * tm * d * 4 + tm * d * 2 + 2 * 3 * d * TF * 2 + 3 * tm * TF * 4
    return pl.pallas_call(
        functools.partial(_ffn_kernel, sub=sub, final=final),
        grid=(t // tm, f // TF),
        in_specs=in_specs,
        out_specs=pl.BlockSpec((tm, d), lambda i, j: (i, 0)),
        out_shape=jax.ShapeDtypeStruct((t, d), F32),
        scratch_shapes=[pltpu.VMEM((tm, d), BF16)],
        compiler_params=_params(("parallel", "arbitrary"), vmem),
        name="ffn",
    )(*args)


def _rope(u, cos, sin, first_half):
    partner = jnp.where(first_half, pltpu.roll(u, 96, 1), pltpu.roll(u, 32, 1))
    return u * cos + partner * sin


def _proj_kernel(x_ref, xn_ref, m_ref, mn_ref, g_ref, w_ref, *rest, sub, tn, n_q_tiles, q_scale,
                 n_rope_tiles):
    if n_rope_tiles:
        cos_ref, sin_ref, o_ref, h_scr = rest
        cos, sin = cos_ref[...], sin_ref[...]
        lane = lax.broadcasted_iota(jnp.int32, cos.shape, 1)
        first_half = (lane % (DH_C // 2)) < (DH_C // 4)
    else:
        o_ref, h_scr = rest
    i = pl.program_id(0)
    slot = i % 2

    @pl.when(i == 0)
    def _():
        h_scr[0] = _adaln(x_ref[...], m_ref, g_ref, sub).astype(BF16)

    n_tiles = o_ref.shape[1] // tn
    for j, (r0, r1) in enumerate(_row_splits(o_ref.shape[0], n_tiles)):
        u = _dot(h_scr[slot], w_ref[:, j * tn:(j + 1) * tn])
        if j < n_q_tiles:
            u = u * q_scale
        h_scr[1 - slot, r0:r1, :] = _adaln(xn_ref[r0:r1, :], mn_ref, g_ref, sub).astype(BF16)
        if j < n_rope_tiles:
            for hh in range(tn // DH_C):
                sl = slice(hh * DH_C, (hh + 1) * DH_C)
                o_ref[:, j * tn + hh * DH_C:j * tn + (hh + 1) * DH_C] = _rope(
                    u[:, sl], cos, sin, first_half).astype(o_ref.dtype)
        else:
            o_ref[:, j * tn:(j + 1) * tn] = u.astype(o_ref.dtype)


def _proj(x, m, norm_g, layer, cond_of_tile, w, widx, sub, out_dtype, tm, n_q_cols, q_scale, rope=None,
          n_rope_cols=0):
    t, d = x.shape
    n = w.shape[-1]
    tn = TN_PROJ
    nt = t // tm
    assert n % tn == 0 and n_q_cols % tn == 0 and n_rope_cols % tn == 0 and TM % tm == 0
    nxt = lambda i: jnp.minimum(i + 1, nt - 1)
    cond = lambda i: cond_of_tile(i * tm // TM)
    once = pl.Buffered(1)
    in_specs = [
        pl.BlockSpec((tm, d), lambda i: (0, 0), pipeline_mode=once),
        pl.BlockSpec((tm, d), lambda i: (nxt(i), 0)),
        pl.BlockSpec((None, None, N_MOD, d), lambda i: (layer, cond(0), 0, 0)),
        pl.BlockSpec((None, None, N_MOD, d), lambda i: (layer, cond(nxt(i)), 0, 0)),
        pl.BlockSpec((None, None, 1, d), lambda i: (layer, sub, 0, 0)),
        pl.BlockSpec((None, d, n), lambda i: (widx, 0, 0), pipeline_mode=once),
    ]
    args = [x, x, m, m, norm_g, w]
    if rope is not None:
        cos, sin = rope
        tiles_per_seq = cos.shape[0] // tm
        in_specs += [pl.BlockSpec((tm, DH_C), lambda i: (i % tiles_per_seq, 0))] * 2
        args += [cos, sin]
    osz = jnp.dtype(out_dtype).itemsize
    vmem = 3 * tm * d * 4 + 2 * tm * d * 2 + d * n * 2 + 2 * tm * n * osz + 4 * tm * tn * 4
    return pl.pallas_call(
        functools.partial(_proj_kernel, sub=sub, tn=tn, n_q_tiles=n_q_cols // tn, q_scale=q_scale,
                          n_rope_tiles=n_rope_cols // tn),
        grid=(nt,),
        in_specs=in_specs,
        out_specs=pl.BlockSpec((tm, n), lambda i: (i, 0)),
        out_shape=jax.ShapeDtypeStruct((t, n), out_dtype),
        scratch_shapes=[pltpu.VMEM((2, tm, d), BF16)],
        compiler_params=_params(("arbitrary",), vmem),
        name="in_proj",
    )(*args)


def _rope_tables(n):
    t = jnp.arange(n)
    nf = DH_C // 4
    inv = ROPE_BASE ** (-jnp.arange(nf, dtype=F32) / nf)
    ang_r = (t // GRID_W).astype(F32)[:, None] * inv[None, :]
    ang_c = (t % GRID_W).astype(F32)[:, None] * inv[None, :]
    cos = jnp.concatenate([jnp.cos(ang_r)] * 2 + [jnp.cos(ang_c)] * 2, axis=-1)
    sin = jnp.concatenate([-jnp.sin(ang_r), jnp.sin(ang_r), -jnp.sin(ang_c), jnp.sin(ang_c)], axis=-1)
    return cos, sin


def _outproj_kernel(*refs, ks):
    ins = refs[:len(ks)]
    w_ref, x_ref, m_ref, o_ref = refs[len(ks):]
    y = None
    off = 0
    for r, k in zip(ins, ks):
        part = _dot(r[...], w_ref[off:off + k, :])
        y = part if y is None else y + part
        off += k
    o_ref[...] = x_ref[...] + m_ref[5:6, :] * y


def _outproj(ins, w, widx, x, m, layer, cond_of_tile):
    t, d = x.shape
    ks = tuple(a.shape[1] for a in ins)
    kin = sum(ks)
    in_specs = [pl.BlockSpec((TM, k), lambda i: (i, 0)) for k in ks] + [
        pl.BlockSpec((None, kin, d), lambda i: (widx, 0, 0)),
        pl.BlockSpec((TM, d), lambda i: (i, 0)),
        pl.BlockSpec((None, None, N_MOD, d), lambda i: (layer, cond_of_tile(i), 0, 0)),
    ]
    vmem = 2 * kin * d * 2 + 2 * TM * kin * 2 + 4 * TM * d * 4 + 2 * TM * d * 4
    return pl.pallas_call(
        functools.partial(_outproj_kernel, ks=ks),
        grid=(t // TM,),
        in_specs=in_specs,
        out_specs=pl.BlockSpec((TM, d), lambda i: (i, 0)),
        out_shape=jax.ShapeDtypeStruct((t, d), F32),
        compiler_params=_params(("parallel",), vmem),
        name="out_proj",
    )(*ins, w, x, m)


def _attn_p_kernel(q_ref, k_ref, v_ref, o_ref, *, n_heads, dh):
    for h in range(n_heads):
        sl = slice(h * dh, (h + 1) * dh)
        s = _dot_nt(q_ref[:, sl].astype(BF16), k_ref[:, sl].astype(BF16))
        e = jnp.exp2(s - jnp.max(s, axis=-1, keepdims=True))
        r = 1.0 / jnp.sum(e, axis=-1, keepdims=True)
        o = _dot(e.astype(BF16), v_ref[:, sl].astype(BF16)) * r
        o_ref[:, sl] = o.astype(o_ref.dtype)


def _attn_prompt(u, n_batch, seq):
    w = H_A * DH_A
    vmem = 2 * 3 * seq * w * 4 + 2 * seq * w * 2 + 8 * seq * seq * 4
    return pl.pallas_call(
        functools.partial(_attn_p_kernel, n_heads=H_A, dh=DH_A),
        grid=(n_batch,),
        in_specs=[pl.BlockSpec((seq, w), lambda b: (b, 0)),
                  pl.BlockSpec((seq, w), lambda b: (b, 1)),
                  pl.BlockSpec((seq, w), lambda b: (b, 2))],
        out_specs=pl.BlockSpec((seq, w), lambda b: (b, 0)),
        out_shape=jax.ShapeDtypeStruct((n_batch * seq, w), BF16),
        compiler_params=_params(("parallel",), vmem),
        name="attn_prompt",
    )(u, u, u)


def _na_bias_table(rpb, rows):
    h = rpb.shape[0]
    kh = min(WIN_H, rows)
    rpb2 = rpb.astype(F32) * LOG2E
    period = GRID_W + 2 * WIN_W - 1
    rp = jnp.pad(rpb2, ((0, 0), (0, 0), (0, GRID_W)), constant_values=NEG_BIG)
    skew = jnp.tile(rp, (1, 1, GRID_W))[:, :, :GRID_W * (period - 1)].reshape(h, -1, GRID_W, period - 1)
    skew = skew[:, :, :, WIN_W - 1:WIN_W - 1 + GRID_W]
    c = np.arange(GRID_W)
    cs = np.clip(c - WIN_W // 2, 0, GRID_W - WIN_W)
    col_ok = (c[None, :] >= cs[:, None]) & (c[None, :] < cs[:, None] + WIN_W)
    tcol = jnp.where(col_ok, skew, NEG_BIG)
    tables = []
    for r0 in (0, NA_QROWS, rows - NA_QROWS):
        base = min(max(r0 - (NA_KROWS - NA_QROWS) // 2, 0), rows - NA_KROWS)
        per_r = []
        for r in range(r0, r0 + NA_QROWS):
            rs = min(max(r - kh // 2, 0), rows - kh)
            i0 = rs - r + WIN_H - 1
            slab = tcol[:, i0:i0 + kh].transpose(0, 2, 1, 3)
            per_r.append(jnp.pad(slab, ((0, 0), (0, 0), (rs - base, NA_KROWS - kh - (rs - base)), (0, 0)),
                                 constant_values=NEG_BIG))
        tables.append(jnp.stack(per_r, axis=1).reshape(h, NA_QROWS * GRID_W, NA_KROWS * GRID_W))
    return jnp.stack(tables)


def _na_kernel(q_ref, k_ref, v_ref, ck_ref, cv_ref, bias_ref, o_ref, ckb_scr, cvb_scr, ec_scr, el_scr,
               r_scr, *, rows):
    j = pl.program_id(2)

    @pl.when(j == 0)
    def _():
        ckb_scr[...] = ck_ref[...].astype(BF16)
        cvb_scr[...] = cv_ref[...].astype(BF16)

    nk = NA_KROWS * GRID_W
    start = jnp.clip(j * NA_QROWS - (NA_KROWS - NA_QROWS) // 2, 0, rows - NA_KROWS) * GRID_W
    start = pl.multiple_of(start, 256)
    q = q_ref[...]
    s_c = _dot_nt(q, ckb_scr[...])
    s_l = _dot_nt(q, k_ref[pl.ds(start, nk), :])
    for rb in range(q.shape[0] // SM_RB):
        rws = slice(rb * SM_RB, (rb + 1) * SM_RB)
        sc = s_c[rws]
        sk = s_l[rws] + bias_ref[rws, :]
        mx = jnp.maximum(jnp.max(sc, axis=-1, keepdims=True), jnp.max(sk, axis=-1, keepdims=True))
        e_c = jnp.exp2(sc - mx)
        e_l = jnp.exp2(sk - mx)
        r_scr[rws, :] = 1.0 / (jnp.sum(e_c, axis=-1, keepdims=True) + jnp.sum(e_l, axis=-1, keepdims=True))
        ec_scr[rws, :] = e_c.astype(BF16)
        el_scr[rws, :] = e_l.astype(BF16)
    o = _dot(ec_scr[...], cvb_scr[...]) + _dot(el_scr[...], v_ref[pl.ds(start, nk), :])
    o_ref[...] = (o * r_scr[...]).astype(o_ref.dtype)


def _na_latent(u, cache_k, cache_v, e, bias, n_batch, n):
    rows = n // GRID_W
    nj = rows // NA_QROWS
    tq = NA_QROWS * GRID_W
    nk = NA_KROWS * GRID_W
    ctx = cache_k.shape[3]
    w = H_A * DH_A

    def bias_idx(b, h, j):
        return ((j > 0).astype(jnp.int32) + (j == nj - 1).astype(jnp.int32), h, 0, 0)

    vmem = (2 * tq * DH_A * 2 * 2 + 2 * 2 * n * DH_A * 2 + 2 * 2 * ctx * DH_A * 4 + 2 * tq * nk * 4
            + 2 * ctx * DH_A * 2 + tq * (nk + ctx) * 2 + 3 * tq * (nk + ctx) * 4)
    return pl.pallas_call(
        functools.partial(_na_kernel, rows=rows),
        grid=(n_batch, H_A, nj),
        in_specs=[
            pl.BlockSpec((tq, DH_A), lambda b, h, j: (b * nj + j, h)),
            pl.BlockSpec((n, DH_A), lambda b, h, j: (b, H_A + h)),
            pl.BlockSpec((n, DH_A), lambda b, h, j: (b, 2 * H_A + h)),
            pl.BlockSpec((None, None, None, ctx, DH_A), lambda b, h, j: (b, e, h, 0, 0)),
            pl.BlockSpec((None, None, None, ctx, DH_A), lambda b, h, j: (b, e, h, 0, 0)),
            pl.BlockSpec((None, None, tq, nk), bias_idx),
        ],
        out_specs=pl.BlockSpec((tq, DH_A), lambda b, h, j: (b * nj + j, h)),
        out_shape=jax.ShapeDtypeStruct((n_batch * n, w), BF16),
        scratch_shapes=[pltpu.VMEM((ctx, DH_A), BF16), pltpu.VMEM((ctx, DH_A), BF16),
                        pltpu.VMEM((tq, ctx), BF16), pltpu.VMEM((tq, nk), BF16),
                        pltpu.VMEM((tq, 1), F32)],
        compiler_params=_params(("parallel", "parallel", "arbitrary"), vmem),
        name="na_latent",
    )(u, u, u, cache_k, cache_v, bias)


def _conv_kernel(a_ref, g_ref, pa_ref, pg_ref, na_ref, ng_ref, w_ref, b_ref, lg_ref, lb_ref, o_ref,
                 z_scr, y_scr, *, tiles_per_seq):
    tm, c = o_ref.shape
    t = pl.program_id(0) % tiles_per_seq

    def glu(a, g):
        return a.astype(F32) * _sigmoid(g.astype(F32))

    z_scr[0:CONV_HALO, :] = jnp.where(t == 0, 0.0, glu(pa_ref[...], pg_ref[...]))
    z_scr[CONV_HALO:CONV_HALO + tm, :] = glu(a_ref[...], g_ref[...])
    z_scr[CONV_HALO + tm:, :] = jnp.where(t == tiles_per_seq - 1, 0.0, glu(na_ref[...], ng_ref[...]))

    first = CONV_HALO - CONV_W // 2
    for cb in range(c // LANES):
        cols = slice(cb * LANES, (cb + 1) * LANES)
        zc = z_scr[:, cols]
        acc = jnp.zeros((tm, LANES), F32)
        for b in range(SUBLANES):
            zb = zc[b:b + tm + 2 * CONV_HALO - SUBLANES]
            for a8 in range(2 * CONV_HALO // SUBLANES):
                k = SUBLANES * a8 + b - first
                if 0 <= k < CONV_W:
                    acc = acc + w_ref[k:k + 1, cols] * zb[SUBLANES * a8:SUBLANES * a8 + tm]
        y_scr[:, cols] = acc + b_ref[:, cols]

    y = y_scr[...]
    mu = jnp.mean(y, axis=-1, keepdims=True)
    yc = y - mu
    yn = yc * lax.rsqrt(jnp.mean(yc * yc, axis=-1, keepdims=True) + EPS) * lg_ref[...] + lb_ref[...]
    o_ref[...] = (yn * _sigmoid(yn)).astype(o_ref.dtype)


def _conv_module(u, col0, c, seq, dw_w, dw_b, ln_g, ln_b):
    t = u.shape[0]
    tm = CONV_TM
    tiles_per_seq = seq // tm
    hb = tm // CONV_HALO
    n_halo = t // CONV_HALO
    ca, cg = col0 // c, col0 // c + 1
    isz = u.dtype.itemsize
    in_specs = [
        pl.BlockSpec((tm, c), lambda i: (i, ca)),
        pl.BlockSpec((tm, c), lambda i: (i, cg)),
        pl.BlockSpec((CONV_HALO, c), lambda i: (jnp.maximum(i * hb - 1, 0), ca)),
        pl.BlockSpec((CONV_HALO, c), lambda i: (jnp.maximum(i * hb - 1, 0), cg)),
        pl.BlockSpec((CONV_HALO, c), lambda i: (jnp.minimum((i + 1) * hb, n_halo - 1), ca)),
        pl.BlockSpec((CONV_HALO, c), lambda i: (jnp.minimum((i + 1) * hb, n_halo - 1), cg)),
        pl.BlockSpec((CONV_W, c), lambda i: (0, 0)),
        pl.BlockSpec((1, c), lambda i: (0, 0)),
        pl.BlockSpec((1, c), lambda i: (0, 0)),
        pl.BlockSpec((1, c), lambda i: (0, 0)),
    ]
    vmem = (4 * (tm + 2 * CONV_HALO) * c * isz + 2 * tm * c * 2 + (2 * tm + 2 * CONV_HALO) * c * 4
            + 4 * CONV_W * c * 4 + 6 * tm * c * 4)
    return pl.pallas_call(
        functools.partial(_conv_kernel, tiles_per_seq=tiles_per_seq),
        grid=(t // tm,),
        in_specs=in_specs,
        out_specs=pl.BlockSpec((tm, c), lambda i: (i, 0)),
        out_shape=jax.ShapeDtypeStruct((t, c), BF16),
        scratch_shapes=[pltpu.VMEM((tm + 2 * CONV_HALO, c), F32), pltpu.VMEM((tm, c), F32)],
        compiler_params=_params(("parallel",), vmem),
        name="conv_module",
    )(u, u, u, u, u, u, dw_w, dw_b.reshape(1, c), ln_g.reshape(1, c), ln_b.reshape(1, c))


def _diff_lambda(lam_ref, lam_init):
    lp = lam_ref[...].astype(F32)
    s01 = jnp.sum(lp[0:1] * lp[1:2], axis=-1, keepdims=True)
    s23 = jnp.sum(lp[2:3] * lp[3:4], axis=-1, keepdims=True)
    return jnp.exp(s01) - jnp.exp(s23) + lam_init


def _subln(o, g_ref, lam_init):
    return o * lax.rsqrt(jnp.mean(o * o, axis=-1, keepdims=True) + EPS) * (g_ref[...] * (1.0 - lam_init))


def _diff_p_kernel(q_ref, k_ref, v_ref, lam_ref, g_ref, o_ref, *, lam_init):
    lam = _diff_lambda(lam_ref, lam_init)
    dv = 2 * DH_C
    for h in range(H_C):
        p = []
        for i in range(2):
            sl = slice((2 * h + i) * DH_C, (2 * h + i + 1) * DH_C)
            s = _dot_nt(q_ref[:, sl].astype(BF16), k_ref[:, sl].astype(BF16))
            e = jnp.exp2(s - jnp.max(s, axis=-1, keepdims=True))
            p.append(e * (1.0 / jnp.sum(e, axis=-1, keepdims=True)))
        a = (p[0] - lam * p[1]).astype(BF16)
        vs = slice(h * dv, (h + 1) * dv)
        o = _dot(a, v_ref[:, vs].astype(BF16))
        o_ref[:, vs] = _subln(o, g_ref, lam_init).astype(o_ref.dtype)


def _diff_prompt(u, lam_p, subln_g, lam_init, n_batch, seq):
    w = 2 * H_C * DH_C
    vmem = 2 * 3 * seq * w * 4 + 2 * seq * w * 2 + 10 * seq * seq * 4
    return pl.pallas_call(
        functools.partial(_diff_p_kernel, lam_init=lam_init),
        grid=(n_batch,),
        in_specs=[pl.BlockSpec((seq, w), lambda b: (b, 0)),
                  pl.BlockSpec((seq, w), lambda b: (b, 1)),
                  pl.BlockSpec((seq, w), lambda b: (b, 2)),
                  pl.BlockSpec((4, DH_C), lambda b: (0, 0)),
                  pl.BlockSpec((1, 2 * DH_C), lambda b: (0, 0))],
        out_specs=pl.BlockSpec((seq, w), lambda b: (b, 0)),
        out_shape=jax.ShapeDtypeStruct((n_batch * seq, w), BF16),
        compiler_params=_params(("parallel",), vmem),
        name="diff_prompt",
    )(u, u, u, lam_p, subln_g.reshape(1, 2 * DH_C))


def _diff_s_kernel(q_ref, k_ref, v_ref, ck_ref, cv_ref, lam_ref, g_ref, o_ref, ckb_scr, cvb_scr, e_scr,
                   acc_scr, *, lam_init):
    @pl.when(pl.program_id(2) == 0)
    def _():
        ckb_scr[...] = ck_ref[...].astype(BF16)
        cvb_scr[...] = cv_ref[...].astype(BF16)

    tq = q_ref.shape[0]
    n_chunks = k_ref.shape[0] // DIFF_KC
    n_rb = tq // SM_RB
    for i in range(2):
        sl = slice(i * DH_C, (i + 1) * DH_C)
        q = q_ref[:, sl]
        mx = [None] * n_rb
        den = [None] * n_rb
        for c in range(n_chunks + 1):
            if c == 0:
                kc, vc = ckb_scr[i], cvb_scr[...]
            else:
                krows = slice((c - 1) * DIFF_KC, c * DIFF_KC)
                kc, vc = k_ref[krows, sl], v_ref[krows, :]
            nkc = kc.shape[0]
            s = _dot_nt(q, kc)
            alpha = [None] * n_rb
            for rb in range(n_rb):
                rws = slice(rb * SM_RB, (rb + 1) * SM_RB)
                sb = s[rws]
                cmax = jnp.max(sb, axis=-1, keepdims=True)
                if c == 0:
                    mx[rb] = cmax
                    e = jnp.exp2(sb - cmax)
                    den[rb] = jnp.sum(e, axis=-1, keepdims=True)
                else:
                    m_new = jnp.maximum(mx[rb], cmax)
                    alpha[rb] = jnp.exp2(mx[rb] - m_new)
                    e = jnp.exp2(sb - m_new)
                    den[rb] = den[rb] * alpha[rb] + jnp.sum(e, axis=-1, keepdims=True)
                    mx[rb] = m_new
                e_scr[rws, :nkc] = e.astype(BF16)
            pv = _dot(e_scr[:, :nkc], vc)
            for rb in range(n_rb):
                rws = slice(rb * SM_RB, (rb + 1) * SM_RB)
                if c == 0:
                    acc_scr[i, rws, :] = pv[rws]
                else:
                    acc_scr[i, rws, :] = acc_scr[i, rws, :] * alpha[rb] + pv[rws]
        for rb in range(n_rb):
            rws = slice(rb * SM_RB, (rb + 1) * SM_RB)
            acc_scr[i, rws, :] = acc_scr[i, rws, :] * (1.0 / den[rb])
    o = acc_scr[0] - _diff_lambda(lam_ref, lam_init) * acc_scr[1]
    o_ref[...] = _subln(o, g_ref, lam_init).astype(o_ref.dtype)


def _diff_latent(u, cache_k, cache_v, o_idx, lam_p, subln_g, lam_init, n_batch, n):
    tq = DIFF_TQ
    nq = n // tq
    ctx = cache_k.shape[3]
    dv = 2 * DH_C
    w = H_C * dv
    nkh = w // dv
    kmax = max(DIFF_KC, ctx)
    vmem = (2 * tq * dv * 2 * 2 + 2 * 2 * n * dv * 2 + 2 * 2 * ctx * dv * 4 + 2 * ctx * dv * 2
            + tq * kmax * 2 + 2 * tq * dv * 4 + 3 * tq * kmax * 4 + 2 * tq * dv * 4)
    return pl.pallas_call(
        functools.partial(_diff_s_kernel, lam_init=lam_init),
        scratch_shapes=[pltpu.VMEM((2, ctx, DH_C), BF16), pltpu.VMEM((ctx, dv), BF16),
                        pltpu.VMEM((tq, kmax), BF16), pltpu.VMEM((2, tq, dv), F32)],
        grid=(n_batch, H_C, nq),
        in_specs=[
            pl.BlockSpec((tq, dv), lambda b, h, i: (b * nq + i, h)),
            pl.BlockSpec((n, dv), lambda b, h, i: (b, nkh + h)),
            pl.BlockSpec((n, dv), lambda b, h, i: (b, 2 * nkh + h)),
            pl.BlockSpec((None, None, 2, ctx, DH_C), lambda b, h, i: (b, o_idx, h, 0, 0)),
            pl.BlockSpec((None, None, None, ctx, dv), lambda b, h, i: (b, o_idx, h, 0, 0)),
            pl.BlockSpec((4, DH_C), lambda b, h, i: (0, 0)),
            pl.BlockSpec((1, dv), lambda b, h, i: (0, 0)),
        ],
        out_specs=pl.BlockSpec((tq, dv), lambda b, h, i: (b * nq + i, h)),
        out_shape=jax.ShapeDtypeStruct((n_batch * n, w), BF16),
        compiler_params=_params(("parallel", "parallel", "arbitrary"), vmem),
        name="diff_latent",
    )(u, u, u, cache_k, cache_v, lam_p, subln_g.reshape(1, dv))


def _split_heads(u2d, n_batch, seq, col0, n_heads, dh):
    t = u2d[:, col0:col0 + n_heads * dh].reshape(n_batch, seq, n_heads, dh)
    return t.transpose(0, 2, 1, 3)


def kernel(x_prompt, x_sample, c, cache_a_k, cache_a_v, cache_c_k, cache_c_v, c_ctx, w_mod, b_mod, norm_g,
           ffn_w1, ffn_w3, ffn_w2, a_w_in, a_w_out, a_rpb, b_dw_w, b_dw_b, b_ln_g, b_ln_b, c_w_in, c_w_out,
           c_lambda, c_subln_g, final_g):
    bp, seq, d = x_prompt.shape
    bs, n, _ = x_sample.shape
    depth = w_mod.shape[0]
    assert 1 + bs <= N_COND and n % TM_FFN == 0 and (bp * seq) % TM_FFN == 0 and TM_FFN % TM == 0

    xp = x_prompt.reshape(bp * seq, d)
    xs = x_sample.reshape(bs * n, d)
    conds = jnp.concatenate([c_ctx[None], c, jnp.zeros((N_COND - 1 - bs, d), F32)], axis=0)
    m = _modulation(conds, w_mod, b_mod)

    tiles_per_latent = n // TM
    cond_p = lambda i: 0
    cond_s = lambda i: 1 + i // tiles_per_latent

    w1 = ffn_w1.astype(BF16)
    w3 = ffn_w3.astype(BF16)
    w2 = ffn_w2.astype(BF16)
    a_in, a_out = a_w_in.astype(BF16), a_w_out.astype(BF16)
    c_in, c_out = c_w_in.astype(BF16), c_w_out.astype(BF16)
    ng = norm_g.reshape(depth, N_SUB, 1, d)
    rope = _rope_tables(n)

    new_a_k, new_a_v, new_c_k, new_c_v = [], [], [], []
    for l in range(depth):
        xp = _ffn(xp, m, ng, l, cond_p, w1, w3, w2, 0, 0)
        xs = _ffn(xs, m, ng, l, cond_s, w1, w3, w2, 0, 0)
        if l % 2 == 0:
            e = l // 2
            wa = H_A * DH_A
            cb = b_dw_w.shape[2]
            conv_p = (b_dw_w[e], b_dw_b[e], b_ln_g[e], b_ln_b[e])
            qs = DH_A ** -0.5 * LOG2E
            up = _proj(xp, m, ng, l, cond_p, a_in, e, 1, F32, TM_F32_OUT, wa, qs)
            us = _proj(xs, m, ng, l, cond_s, a_in, e, 1, BF16, TM, wa, qs)
            new_a_k.append(_split_heads(up, bp, seq, wa, H_A, DH_A))
            new_a_v.append(_split_heads(up, bp, seq, 2 * wa, H_A, DH_A))
            op = _attn_prompt(up, bp, seq)
            cp = _conv_module(up, 3 * wa, cb, seq, *conv_p)
            bias = _na_bias_table(a_rpb[e], n // GRID_W)
            os_ = _na_latent(us, cache_a_k, cache_a_v, e, bias, bs, n)
            cs = _conv_module(us, 3 * wa, cb, n, *conv_p)
            xp = _outproj([op, cp], a_out, e, xp, m, l, cond_p)
            xs = _outproj([os_, cs], a_out, e, xs, m, l, cond_s)
        else:
            o = l // 2
            lam_init = 0.8 - 0.6 * math.exp(-0.3 * l)
            wqk = 2 * H_C * DH_C
            qs = DH_C ** -0.5 * LOG2E
            up = _proj(xp, m, ng, l, cond_p, c_in, o, 1, F32, TM_F32_OUT, wqk, qs)
            us = _proj(xs, m, ng, l, cond_s, c_in, o, 1, BF16, TM, wqk, qs, rope=rope, n_rope_cols=2 * wqk)
            new_c_k.append(_split_heads(up, bp, seq, wqk, 2 * H_C, DH_C))
            new_c_v.append(_split_heads(up, bp, seq, 2 * wqk, H_C, 2 * DH_C))
            op = _diff_prompt(up, c_lambda[o], c_subln_g[o], lam_init, bp, seq)
            os_ = _diff_latent(us, cache_c_k, cache_c_v, o, c_lambda[o], c_subln_g[o], lam_init, bs, n)
            xp = _outproj([op], c_out, o, xp, m, l, cond_p)
            xs = _outproj([os_], c_out, o, xs, m, l, cond_s)
        fg = final_g if l == depth - 1 else None
        xp = _ffn(xp, m, ng, l, cond_p, w1, w3, w2, 1, 2, final_g=fg)
        xs = _ffn(xs, m, ng, l, cond_s, w1, w3, w2, 1, 2, final_g=fg)

    return (xp.reshape(bp, seq, d), xs.reshape(bs, n, d),
            jnp.stack(new_a_k, axis=1), jnp.stack(new_a_v, axis=1),
            jnp.stack(new_c_k, axis=1), jnp.stack(new_c_v, axis=1))
```

```python
import functools
import math

import jax
import jax.numpy as jnp
import numpy as np
from jax import lax
from jax.experimental import pallas as pl
from jax.experimental.pallas import tpu as pltpu

F32 = jnp.float32
BF16 = jnp.bfloat16

GRID_W = 64
H_A = 8
DH_A = 128
WIN_H = 8
WIN_W = 16
CONV_W = 31
H_C = 8
DH_C = 128
N_SUB = 3
N_MOD = 3 * N_SUB
EPS = 1e-6
ROPE_BASE = 10000.0
LOG2E = 1.0 / math.log(2.0)

LANES = 128
SUBLANES = 8
BF16_ROWS = 16
VMEM_CAP_BYTES = 64 * 1024 * 1024
NEG_BIG = -1e30

N_COND = 8
TM = 512
TM_FFN = 1024
TF = 512
TN_PROJ = 1024
TN_MOD = 1024
CONV_TM = 128
CONV_HALO = 16
NA_QROWS = 8
NA_KROWS = 16
DIFF_TQ = 512
DIFF_KC = 1024
SM_RB = 32


def _params(sem, vmem_bytes):
    limit = min(int(vmem_bytes * 1.25) + (4 << 20), VMEM_CAP_BYTES - (6 << 20))
    return pltpu.CompilerParams(dimension_semantics=sem, vmem_limit_bytes=limit)


def _sigmoid(x):
    return 1.0 / (1.0 + jnp.exp(-x))


def _dot(a, b):
    return jnp.dot(a, b, preferred_element_type=F32)


def _dot_nt(a, b):
    return lax.dot_general(a, b, (((1,), (1,)), ((), ())), preferred_element_type=F32)


def _adaln(x, m_ref, g_ref, sub):
    r = lax.rsqrt(jnp.mean(x * x, axis=-1, keepdims=True) + EPS)
    a = g_ref[...] * (1.0 + m_ref[3 * sub + 1:3 * sub + 2, :])
    return x * r * a + m_ref[3 * sub:3 * sub + 1, :]


def _row_splits(tm, n_parts):
    cuts = [round(k * tm / n_parts / BF16_ROWS) * BF16_ROWS for k in range(n_parts + 1)]
    return list(zip(cuts[:-1], cuts[1:]))


def _mod_kernel(c_ref, w_ref, b_ref, o_ref):
    c = c_ref[...]
    s = (c * _sigmoid(c)).astype(BF16)
    o_ref[...] = _dot(s, w_ref[...].astype(BF16)) + b_ref[...]


def _modulation(conds, w_mod, b_mod):
    depth, d, n = w_mod.shape
    tn = TN_MOD
    vmem = 2 * d * tn * 4 + d * tn * 2 + 4 * N_COND * tn * 4
    out = pl.pallas_call(
        _mod_kernel,
        grid=(depth, n // tn),
        in_specs=[
            pl.BlockSpec((N_COND, d), lambda l, j: (0, 0)),
            pl.BlockSpec((None, d, tn), lambda l, j: (l, 0, j)),
            pl.BlockSpec((None, 1, tn), lambda l, j: (l, 0, j)),
        ],
        out_specs=pl.BlockSpec((None, N_COND, tn), lambda l, j: (l, 0, j)),
        out_shape=jax.ShapeDtypeStruct((depth, N_COND, n), F32),
        compiler_params=_params(("arbitrary", "arbitrary"), vmem),
        name="modulation",
    )(conds, w_mod, b_mod.reshape(depth, 1, n))
    return out.reshape(depth, N_COND, N_MOD, d)


def _ffn_kernel(x_ref, m_ref, g_ref, w1_ref, w3_ref, w2_ref, *rest, sub, final):
    if final:
        fg_ref, o_ref, h_scr = rest
    else:
        o_ref, h_scr = rest
    j = pl.program_id(1)

    def dff_tile():
        h = h_scr[...]
        a = _dot(h, w1_ref[...])
        b = _dot(h, w3_ref[...])
        return _dot((a * _sigmoid(a) * b).astype(BF16), w2_ref[...])

    @pl.when(j == 0)
    def _():
        h_scr[...] = _adaln(x_ref[...], m_ref, g_ref, sub).astype(BF16)
        o_ref[...] = dff_tile()

    @pl.when(j > 0)
    def _():
        o_ref[...] += dff_tile()

    @pl.when(j == pl.num_programs(1) - 1)
    def _():
        y = x_ref[...] + (0.5 * m_ref[3 * sub + 2:3 * sub + 3, :]) * o_ref[...]
        if final:
            y = y * lax.rsqrt(jnp.mean(y * y, axis=-1, keepdims=True) + EPS) * fg_ref[...]
        o_ref[...] = y


def _ffn(x, m, norm_g, layer, cond_of_tile, w1, w3, w2, which, sub, final_g=None):
    t, d = x.shape
    f = w1.shape[-1]
    tm = TM_FFN
    final = final_g is not None
    in_specs = [
        pl.BlockSpec((tm, d), lambda i, j: (i, 0)),
        pl.BlockSpec((None, None, N_MOD, d), lambda i, j: (layer, cond_of_tile(i * (tm // TM)), 0, 0)),
        pl.BlockSpec((None, None, 1, d), lambda i, j: (layer, sub, 0, 0)),
        pl.BlockSpec((None, None, d, TF), lambda i, j: (layer, which, 0, j)),
        pl.BlockSpec((None, None, d, TF), lambda i, j: (layer, which, 0, j)),
        pl.BlockSpec((None, None, TF, d), lambda i, j: (layer, which, j, 0)),
    ]
    args = [x, m, norm_g, w1, w3, w2]
    if final:
        in_specs.append(pl.BlockSpec((1, d), lambda i, j: (0, 0)))
        args.append(final_g.reshape(1, d))
    vmem = 4 * tm * d * 4 + tm * d * 2 + 2 * 3 * d * TF * 2 + 3 * tm * TF * 4
    return pl.pallas_call(
        functools.partial(_ffn_kernel, sub=sub, final=final),
        grid=(t // tm, f // TF),
        in_specs=in_specs,
        out_specs=pl.BlockSpec((tm, d), lambda i, j: (i, 0)),
        out_shape=jax.ShapeDtypeStruct((t, d), F32),
        scratch_shapes=[pltpu.VMEM((tm, d), BF16)],
        compiler_params=_params(("parallel", "arbitrary"), vmem),
        name="ffn",
    )(*args)


def _rope(u, cos, sin, first_half):
    partner = jnp.where(first_half, pltpu.roll(u, 96, 1), pltpu.roll(u, 32, 1))
    return u * cos + partner * sin


def _proj_kernel(x_ref, xn_ref, m_ref, mn_ref, g_ref, w_ref, *rest, sub, tn, n_q_tiles, q_scale,
                 n_rope_tiles, head_outs):
    if n_rope_tiles:
        cos_ref, sin_ref = rest[:2]
        rest = rest[2:]
        cos, sin = cos_ref[...], sin_ref[...]
        lane = lax.broadcasted_iota(jnp.int32, cos.shape, 1)
        first_half = (lane % (DH_C // 2)) < (DH_C // 4)
    o_ref, head_refs, h_scr = rest[0], rest[1:-1], rest[-1]
    i = pl.program_id(0)
    slot = i % 2

    @pl.when(i == 0)
    def _():
        h_scr[0] = _adaln(x_ref[...], m_ref, g_ref, sub).astype(BF16)

    n_tiles = o_ref.shape[1] // tn
    for j, (r0, r1) in enumerate(_row_splits(o_ref.shape[0], n_tiles)):
        u = _dot(h_scr[slot], w_ref[:, j * tn:(j + 1) * tn])
        if j < n_q_tiles:
            u = u * q_scale
        h_scr[1 - slot, r0:r1, :] = _adaln(xn_ref[r0:r1, :], mn_ref, g_ref, sub).astype(BF16)
        if j < n_rope_tiles:
            for hh in range(tn // DH_C):
                sl = slice(hh * DH_C, (hh + 1) * DH_C)
                o_ref[:, j * tn + hh * DH_C:j * tn + (hh + 1) * DH_C] = _rope(
                    u[:, sl], cos, sin, first_half).astype(o_ref.dtype)
        else:
            o_ref[:, j * tn:(j + 1) * tn] = u.astype(o_ref.dtype)
        for ref, (col0, n_heads, dh) in zip(head_refs, head_outs):
            for hh in range(n_heads):
                c0 = col0 + hh * dh - j * tn
                if 0 <= c0 < tn:
                    ref[hh] = u[:, c0:c0 + dh]


def _proj(x, m, norm_g, layer, cond_of_tile, w, widx, sub, out_dtype, tm, n_q_cols, q_scale, rope=None,
          n_rope_cols=0, head_outs=()):
    t, d = x.shape
    n = w.shape[-1]
    tn = TN_PROJ
    nt = t // tm
    assert n % tn == 0 and n_q_cols % tn == 0 and n_rope_cols % tn == 0 and TM % tm == 0
    assert all(c0 >= n_q_cols and c0 % dh == 0 and tn % dh == 0 for c0, _, dh in head_outs)
    nxt = lambda i: jnp.minimum(i + 1, nt - 1)
    cond = lambda i: cond_of_tile(i * tm // TM)
    once = pl.Buffered(1)
    in_specs = [
        pl.BlockSpec((tm, d), lambda i: (0, 0), pipeline_mode=once),
        pl.BlockSpec((tm, d), lambda i: (nxt(i), 0)),
        pl.BlockSpec((None, None, N_MOD, d), lambda i: (layer, cond(0), 0, 0)),
        pl.BlockSpec((None, None, N_MOD, d), lambda i: (layer, cond(nxt(i)), 0, 0)),
        pl.BlockSpec((None, None, 1, d), lambda i: (layer, sub, 0, 0)),
        pl.BlockSpec((None, d, n), lambda i: (widx, 0, 0), pipeline_mode=once),
    ]
    args = [x, x, m, m, norm_g, w]
    if rope is not None:
        cos, sin = rope
        tiles_per_seq = cos.shape[0] // tm
        in_specs += [pl.BlockSpec((tm, DH_C), lambda i: (i % tiles_per_seq, 0))] * 2
        args += [cos, sin]
    out_specs = [pl.BlockSpec((tm, n), lambda i: (i, 0))]
    out_shape = [jax.ShapeDtypeStruct((t, n), out_dtype)]
    for _, n_heads, dh in head_outs:
        out_specs.append(pl.BlockSpec((None, n_heads, tm, dh), lambda i: (i, 0, 0, 0)))
        out_shape.append(jax.ShapeDtypeStruct((nt, n_heads, tm, dh), F32))
    osz = jnp.dtype(out_dtype).itemsize
    vmem = (3 * tm * d * 4 + 2 * tm * d * 2 + d * n * 2 + 2 * tm * n * osz + 4 * tm * tn * 4
            + sum(2 * n_heads * tm * dh * 4 for _, n_heads, dh in head_outs))
    return pl.pallas_call(
        functools.partial(_proj_kernel, sub=sub, tn=tn, n_q_tiles=n_q_cols // tn, q_scale=q_scale,
                          n_rope_tiles=n_rope_cols // tn, head_outs=tuple(head_outs)),
        grid=(nt,),
        in_specs=in_specs,
        out_specs=out_specs,
        out_shape=out_shape,
        scratch_shapes=[pltpu.VMEM((2, tm, d), BF16)],
        compiler_params=_params(("arbitrary",), vmem),
        name="in_proj",
    )(*args)


def _rope_tables(n):
    t = jnp.arange(n)
    nf = DH_C // 4
    inv = ROPE_BASE ** (-jnp.arange(nf, dtype=F32) / nf)
    ang_r = (t // GRID_W).astype(F32)[:, None] * inv[None, :]
    ang_c = (t % GRID_W).astype(F32)[:, None] * inv[None, :]
    cos = jnp.concatenate([jnp.cos(ang_r)] * 2 + [jnp.cos(ang_c)] * 2, axis=-1)
    sin = jnp.concatenate([-jnp.sin(ang_r), jnp.sin(ang_r), -jnp.sin(ang_c), jnp.sin(ang_c)], axis=-1)
    return cos, sin


def _outproj_kernel(*refs, ks):
    ins = refs[:len(ks)]
    w_ref, x_ref, m_ref, o_ref = refs[len(ks):]
    y = None
    off = 0
    for r, k in zip(ins, ks):
        part = _dot(r[...], w_ref[off:off + k, :])
        y = part if y is None else y + part
        off += k
    o_ref[...] = x_ref[...] + m_ref[5:6, :] * y


def _outproj(ins, w, widx, x, m, layer, cond_of_tile):
    t, d = x.shape
    ks = tuple(a.shape[1] for a in ins)
    kin = sum(ks)
    in_specs = [pl.BlockSpec((TM, k), lambda i: (i, 0)) for k in ks] + [
        pl.BlockSpec((None, kin, d), lambda i: (widx, 0, 0)),
        pl.BlockSpec((TM, d), lambda i: (i, 0)),
        pl.BlockSpec((None, None, N_MOD, d), lambda i: (layer, cond_of_tile(i), 0, 0)),
    ]
    vmem = 2 * kin * d * 2 + 2 * TM * kin * 2 + 4 * TM * d * 4 + 2 * TM * d * 4
    return pl.pallas_call(
        functools.partial(_outproj_kernel, ks=ks),
        grid=(t // TM,),
        in_specs=in_specs,
        out_specs=pl.BlockSpec((TM, d), lambda i: (i, 0)),
        out_shape=jax.ShapeDtypeStruct((t, d), F32),
        compiler_params=_params(("parallel",), vmem),
        name="out_proj",
    )(*ins, w, x, m)


def _attn_p_kernel(q_ref, k_ref, v_ref, o_ref, *, n_heads, dh):
    for h in range(n_heads):
        sl = slice(h * dh, (h + 1) * dh)
        s = _dot_nt(q_ref[:, sl].astype(BF16), k_ref[:, sl].astype(BF16))
        e = jnp.exp2(s - jnp.max(s, axis=-1, keepdims=True))
        r = 1.0 / jnp.sum(e, axis=-1, keepdims=True)
        o = _dot(e.astype(BF16), v_ref[:, sl].astype(BF16)) * r
        o_ref[:, sl] = o.astype(o_ref.dtype)


def _attn_prompt(u, n_batch, seq):
    w = H_A * DH_A
    vmem = 2 * 3 * seq * w * 4 + 2 * seq * w * 2 + 8 * seq * seq * 4
    return pl.pallas_call(
        functools.partial(_attn_p_kernel, n_heads=H_A, dh=DH_A),
        grid=(n_batch,),
        in_specs=[pl.BlockSpec((seq, w), lambda b: (b, 0)),
                  pl.BlockSpec((seq, w), lambda b: (b, 1)),
                  pl.BlockSpec((seq, w), lambda b: (b, 2))],
        out_specs=pl.BlockSpec((seq, w), lambda b: (b, 0)),
        out_shape=jax.ShapeDtypeStruct((n_batch * seq, w), BF16),
        compiler_params=_params(("parallel",), vmem),
        name="attn_prompt",
    )(u, u, u)


def _na_col_table(rpb):
    h = rpb.shape[0]
    rpb2 = rpb.astype(F32) * LOG2E
    period = GRID_W + 2 * WIN_W - 1
    rp = jnp.pad(rpb2, ((0, 0), (0, 0), (0, GRID_W)), constant_values=NEG_BIG)
    skew = jnp.tile(rp, (1, 1, GRID_W))[:, :, :GRID_W * (period - 1)].reshape(h, -1, GRID_W, period - 1)
    skew = skew[:, :, :, WIN_W - 1:WIN_W - 1 + GRID_W]
    c = np.arange(GRID_W)
    cs = np.clip(c - WIN_W // 2, 0, GRID_W - WIN_W)
    col_ok = (c[None, :] >= cs[:, None]) & (c[None, :] < cs[:, None] + WIN_W)
    tcol = jnp.where(col_ok, skew, NEG_BIG)
    return jnp.concatenate([tcol, tcol], axis=-1)


def _na_block_types(rows):
    return [(r0, min(max(r0 - (NA_KROWS - NA_QROWS) // 2, 0), rows - NA_KROWS))
            for r0 in (0, NA_QROWS, rows - NA_QROWS)]


def _na_build_bias(tc_ref, bias_scr, rows):
    kh = min(WIN_H, rows)
    lane = lax.broadcasted_iota(jnp.int32, (GRID_W, LANES), 1)
    left = lane < GRID_W
    neg = jnp.full((GRID_W, LANES), NEG_BIG, F32)
    for ty, (r0, base) in enumerate(_na_block_types(rows)):
        for rr in range(NA_QROWS):
            r = r0 + rr
            rs = min(max(r - kh // 2, 0), rows - kh)
            for p in range(NA_KROWS * GRID_W // LANES):
                halves = []
                for kr in (base + 2 * p, base + 2 * p + 1):
                    halves.append(tc_ref[kr - r + WIN_H - 1] if rs <= kr < rs + kh else None)
                if halves[0] is None and halves[1] is None:
                    tile = neg
                else:
                    tile = jnp.where(left, neg if halves[0] is None else halves[0],
                                     neg if halves[1] is None else halves[1])
                bias_scr[ty, rr * GRID_W:(rr + 1) * GRID_W, p * LANES:(p + 1) * LANES] = tile


def _na_kernel(q_ref, k_ref, v_ref, ck_ref, cv_ref, tc_ref, o_ref, ckb_scr, cvb_scr, bias_scr, ec_scr,
               el_scr, r_scr, *, rows):
    j = pl.program_id(2)
    nj = pl.num_programs(2)

    @pl.when(j == 0)
    def _():
        ckb_scr[...] = ck_ref[...].astype(BF16)
        cvb_scr[...] = cv_ref[...].astype(BF16)
        _na_build_bias(tc_ref, bias_scr, rows)

    bias_ref = bias_scr.at[(j > 0).astype(jnp.int32) + (j == nj - 1).astype(jnp.int32)]
    nk = NA_KROWS * GRID_W
    start = jnp.clip(j * NA_QROWS - (NA_KROWS - NA_QROWS) // 2, 0, rows - NA_KROWS) * GRID_W
    start = pl.multiple_of(start, 256)
    q = q_ref[...]
    s_c = _dot_nt(q, ckb_scr[...])
    s_l = _dot_nt(q, k_ref[pl.ds(start, nk), :])
    for rb in range(q.shape[0] // SM_RB):
        rws = slice(rb * SM_RB, (rb + 1) * SM_RB)
        sc = s_c[rws]
        sk = s_l[rws] + bias_ref[rws, :]
        mx = jnp.maximum(jnp.max(sc, axis=-1, keepdims=True), jnp.max(sk, axis=-1, keepdims=True))
        e_c = jnp.exp2(sc - mx)
        e_l = jnp.exp2(sk - mx)
        r_scr[rws, :] = 1.0 / (jnp.sum(e_c, axis=-1, keepdims=True) + jnp.sum(e_l, axis=-1, keepdims=True))
        ec_scr[rws, :] = e_c.astype(BF16)
        el_scr[rws, :] = e_l.astype(BF16)
    o = _dot(ec_scr[...], cvb_scr[...]) + _dot(el_scr[...], v_ref[pl.ds(start, nk), :])
    o_ref[...] = (o * r_scr[...]).astype(o_ref.dtype)


def _na_latent(u, cache_k, cache_v, e, col_table, n_batch, n):
    rows = n // GRID_W
    nj = rows // NA_QROWS
    tq = NA_QROWS * GRID_W
    nk = NA_KROWS * GRID_W
    ctx = cache_k.shape[3]
    w = H_A * DH_A
    n_rel = col_table.shape[1]
    assert 2 * GRID_W == LANES and nj >= 3
    vmem = (2 * tq * DH_A * 2 * 2 + 2 * 2 * n * DH_A * 2 + 2 * 2 * ctx * DH_A * 4 + 3 * tq * nk * 4
            + 2 * n_rel * GRID_W * LANES * 4
            + 2 * ctx * DH_A * 2 + tq * (nk + ctx) * 2 + 3 * tq * (nk + ctx) * 4)
    return pl.pallas_call(
        functools.partial(_na_kernel, rows=rows),
        grid=(n_batch, H_A, nj),
        in_specs=[
            pl.BlockSpec((tq, DH_A), lambda b, h, j: (b * nj + j, h)),
            pl.BlockSpec((n, DH_A), lambda b, h, j: (b, H_A + h)),
            pl.BlockSpec((n, DH_A), lambda b, h, j: (b, 2 * H_A + h)),
            pl.BlockSpec((None, None, None, ctx, DH_A), lambda b, h, j: (b, e, h, 0, 0)),
            pl.BlockSpec((None, None, None, ctx, DH_A), lambda b, h, j: (b, e, h, 0, 0)),
            pl.BlockSpec((None, n_rel, GRID_W, LANES), lambda b, h, j: (h, 0, 0, 0)),
        ],
        out_specs=pl.BlockSpec((tq, DH_A), lambda b, h, j: (b * nj + j, h)),
        out_shape=jax.ShapeDtypeStruct((n_batch * n, w), BF16),
        scratch_shapes=[pltpu.VMEM((ctx, DH_A), BF16), pltpu.VMEM((ctx, DH_A), BF16),
                        pltpu.VMEM((3, tq, nk), F32),
                        pltpu.VMEM((tq, ctx), BF16), pltpu.VMEM((tq, nk), BF16),
                        pltpu.VMEM((tq, 1), F32)],
        compiler_params=_params(("parallel", "parallel", "arbitrary"), vmem),
        name="na_latent",
    )(u, u, u, cache_k, cache_v, col_table)


def _conv_kernel(a_ref, g_ref, pa_ref, pg_ref, na_ref, ng_ref, w_ref, b_ref, lg_ref, lb_ref, o_ref,
                 z_scr, y_scr, *, tiles_per_seq):
    tm, c = o_ref.shape
    t = pl.program_id(0) % tiles_per_seq

    def glu(a, g):
        return a.astype(F32) * _sigmoid(g.astype(F32))

    z_scr[0:CONV_HALO, :] = jnp.where(t == 0, 0.0, glu(pa_ref[...], pg_ref[...]))
    z_scr[CONV_HALO:CONV_HALO + tm, :] = glu(a_ref[...], g_ref[...])
    z_scr[CONV_HALO + tm:, :] = jnp.where(t == tiles_per_seq - 1, 0.0, glu(na_ref[...], ng_ref[...]))

    first = CONV_HALO - CONV_W // 2
    for cb in range(c // LANES):
        cols = slice(cb * LANES, (cb + 1) * LANES)
        zc = z_scr[:, cols]
        acc = jnp.zeros((tm, LANES), F32)
        for b in range(SUBLANES):
            zb = zc[b:b + tm + 2 * CONV_HALO - SUBLANES]
            for a8 in range(2 * CONV_HALO // SUBLANES):
                k = SUBLANES * a8 + b - first
                if 0 <= k < CONV_W:
                    acc = acc + w_ref[k:k + 1, cols] * zb[SUBLANES * a8:SUBLANES * a8 + tm]
        y_scr[:, cols] = acc + b_ref[:, cols]

    y = y_scr[...]
    mu = jnp.mean(y, axis=-1, keepdims=True)
    yc = y - mu
    yn = yc * lax.rsqrt(jnp.mean(yc * yc, axis=-1, keepdims=True) + EPS) * lg_ref[...] + lb_ref[...]
    o_ref[...] = (yn * _sigmoid(yn)).astype(o_ref.dtype)


def _conv_module(u, col0, c, seq, dw_w, dw_b, ln_g, ln_b):
    t = u.shape[0]
    tm = CONV_TM
    tiles_per_seq = seq // tm
    hb = tm // CONV_HALO
    n_halo = t // CONV_HALO
    ca, cg = col0 // c, col0 // c + 1
    isz = u.dtype.itemsize
    in_specs = [
        pl.BlockSpec((tm, c), lambda i: (i, ca)),
        pl.BlockSpec((tm, c), lambda i: (i, cg)),
        pl.BlockSpec((CONV_HALO, c), lambda i: (jnp.maximum(i * hb - 1, 0), ca)),
        pl.BlockSpec((CONV_HALO, c), lambda i: (jnp.maximum(i * hb - 1, 0), cg)),
        pl.BlockSpec((CONV_HALO, c), lambda i: (jnp.minimum((i + 1) * hb, n_halo - 1), ca)),
        pl.BlockSpec((CONV_HALO, c), lambda i: (jnp.minimum((i + 1) * hb, n_halo - 1), cg)),
        pl.BlockSpec((CONV_W, c), lambda i: (0, 0)),
        pl.BlockSpec((1, c), lambda i: (0, 0)),
        pl.BlockSpec((1, c), lambda i: (0, 0)),
        pl.BlockSpec((1, c), lambda i: (0, 0)),
    ]
    vmem = (4 * (tm + 2 * CONV_HALO) * c * isz + 2 * tm * c * 2 + (2 * tm + 2 * CONV_HALO) * c * 4
            + 4 * CONV_W * c * 4 + 6 * tm * c * 4)
    return pl.pallas_call(
        functools.partial(_conv_kernel, tiles_per_seq=tiles_per_seq),
        grid=(t // tm,),
        in_specs=in_specs,
        out_specs=pl.BlockSpec((tm, c), lambda i: (i, 0)),
        out_shape=jax.ShapeDtypeStruct((t, c), BF16),
        scratch_shapes=[pltpu.VMEM((tm + 2 * CONV_HALO, c), F32), pltpu.VMEM((tm, c), F32)],
        compiler_params=_params(("parallel",), vmem),
        name="conv_module",
    )(u, u, u, u, u, u, dw_w, dw_b.reshape(1, c), ln_g.reshape(1, c), ln_b.reshape(1, c))


def _diff_lambda(lam_ref, lam_init):
    lp = lam_ref[...].astype(F32)
    s01 = jnp.sum(lp[0:1] * lp[1:2], axis=-1, keepdims=True)
    s23 = jnp.sum(lp[2:3] * lp[3:4], axis=-1, keepdims=True)
    return jnp.exp(s01) - jnp.exp(s23) + lam_init


def _subln(o, g_ref, lam_init):
    return o * lax.rsqrt(jnp.mean(o * o, axis=-1, keepdims=True) + EPS) * (g_ref[...] * (1.0 - lam_init))


def _diff_p_kernel(q_ref, k_ref, v_ref, lam_ref, g_ref, o_ref, *, lam_init):
    lam = _diff_lambda(lam_ref, lam_init)
    dv = 2 * DH_C
    for h in range(H_C):
        p = []
        for i in range(2):
            sl = slice((2 * h + i) * DH_C, (2 * h + i + 1) * DH_C)
            s = _dot_nt(q_ref[:, sl].astype(BF16), k_ref[:, sl].astype(BF16))
            e = jnp.exp2(s - jnp.max(s, axis=-1, keepdims=True))
            p.append(e * (1.0 / jnp.sum(e, axis=-1, keepdims=True)))
        a = (p[0] - lam * p[1]).astype(BF16)
        vs = slice(h * dv, (h + 1) * dv)
        o = _dot(a, v_ref[:, vs].astype(BF16))
        o_ref[:, vs] = _subln(o, g_ref, lam_init).astype(o_ref.dtype)


def _diff_prompt(u, lam_p, subln_g, lam_init, n_batch, seq):
    w = 2 * H_C * DH_C
    vmem = 2 * 3 * seq * w * 4 + 2 * seq * w * 2 + 10 * seq * seq * 4
    return pl.pallas_call(
        functools.partial(_diff_p_kernel, lam_init=lam_init),
        grid=(n_batch,),
        in_specs=[pl.BlockSpec((seq, w), lambda b: (b, 0)),
                  pl.BlockSpec((seq, w), lambda b: (b, 1)),
                  pl.BlockSpec((seq, w), lambda b: (b, 2)),
                  pl.BlockSpec((4, DH_C), lambda b: (0, 0)),
                  pl.BlockSpec((1, 2 * DH_C), lambda b: (0, 0))],
        out_specs=pl.BlockSpec((seq, w), lambda b: (b, 0)),
        out_shape=jax.ShapeDtypeStruct((n_batch * seq, w), BF16),
        compiler_params=_params(("parallel",), vmem),
        name="diff_prompt",
    )(u, u, u, lam_p, subln_g.reshape(1, 2 * DH_C))


def _diff_s_kernel(q_ref, k_ref, v_ref, ck_ref, cv_ref, lam_ref, g_ref, o_ref, ckb_scr, cvb_scr, e_scr,
                   acc_scr, *, lam_init):
    @pl.when(pl.program_id(2) == 0)
    def _():
        ckb_scr[...] = ck_ref[...].astype(BF16)
        cvb_scr[...] = cv_ref[...].astype(BF16)

    tq = q_ref.shape[0]
    n_chunks = k_ref.shape[0] // DIFF_KC
    n_rb = tq // SM_RB
    for i in range(2):
        sl = slice(i * DH_C, (i + 1) * DH_C)
        q = q_ref[:, sl]
        mx = [None] * n_rb
        den = [None] * n_rb
        for c in range(n_chunks + 1):
            if c == 0:
                kc, vc = ckb_scr[i], cvb_scr[...]
            else:
                krows = slice((c - 1) * DIFF_KC, c * DIFF_KC)
                kc, vc = k_ref[krows, sl], v_ref[krows, :]
            nkc = kc.shape[0]
            s = _dot_nt(q, kc)
            alpha = [None] * n_rb
            for rb in range(n_rb):
                rws = slice(rb * SM_RB, (rb + 1) * SM_RB)
                sb = s[rws]
                cmax = jnp.max(sb, axis=-1, keepdims=True)
                if c == 0:
                    mx[rb] = cmax
                    e = jnp.exp2(sb - cmax)
                    den[rb] = jnp.sum(e, axis=-1, keepdims=True)
                else:
                    m_new = jnp.maximum(mx[rb], cmax)
                    alpha[rb] = jnp.exp2(mx[rb] - m_new)
                    e = jnp.exp2(sb - m_new)
                    den[rb] = den[rb] * alpha[rb] + jnp.sum(e, axis=-1, keepdims=True)
                    mx[rb] = m_new
                e_scr[rws, :nkc] = e.astype(BF16)
            pv = _dot(e_scr[:, :nkc], vc)
            for rb in range(n_rb):
                rws = slice(rb * SM_RB, (rb + 1) * SM_RB)
                if c == 0:
                    acc_scr[i, rws, :] = pv[rws]
                else:
                    acc_scr[i, rws, :] = acc_scr[i, rws, :] * alpha[rb] + pv[rws]
        for rb in range(n_rb):
            rws = slice(rb * SM_RB, (rb + 1) * SM_RB)
            acc_scr[i, rws, :] = acc_scr[i, rws, :] * (1.0 / den[rb])
    o = acc_scr[0] - _diff_lambda(lam_ref, lam_init) * acc_scr[1]
    o_ref[...] = _subln(o, g_ref, lam_init).astype(o_ref.dtype)


def _diff_latent(u, cache_k, cache_v, o_idx, lam_p, subln_g, lam_init, n_batch, n):
    tq = DIFF_TQ
    nq = n // tq
    ctx = cache_k.shape[3]
    dv = 2 * DH_C
    w = H_C * dv
    nkh = w // dv
    kmax = max(DIFF_KC, ctx)
    vmem = (2 * tq * dv * 2 * 2 + 2 * 2 * n * dv * 2 + 2 * 2 * ctx * dv * 4 + 2 * ctx * dv * 2
            + tq * kmax * 2 + 2 * tq * dv * 4 + 3 * tq * kmax * 4 + 2 * tq * dv * 4)
    return pl.pallas_call(
        functools.partial(_diff_s_kernel, lam_init=lam_init),
        scratch_shapes=[pltpu.VMEM((2, ctx, DH_C), BF16), pltpu.VMEM((ctx, dv), BF16),
                        pltpu.VMEM((tq, kmax), BF16), pltpu.VMEM((2, tq, dv), F32)],
        grid=(n_batch, H_C, nq),
        in_specs=[
            pl.BlockSpec((tq, dv), lambda b, h, i: (b * nq + i, h)),
            pl.BlockSpec((n, dv), lambda b, h, i: (b, nkh + h)),
            pl.BlockSpec((n, dv), lambda b, h, i: (b, 2 * nkh + h)),
            pl.BlockSpec((None, None, 2, ctx, DH_C), lambda b, h, i: (b, o_idx, h, 0, 0)),
            pl.BlockSpec((None, None, None, ctx, dv), lambda b, h, i: (b, o_idx, h, 0, 0)),
            pl.BlockSpec((4, DH_C), lambda b, h, i: (0, 0)),
            pl.BlockSpec((1, dv), lambda b, h, i: (0, 0)),
        ],
        out_specs=pl.BlockSpec((tq, dv), lambda b, h, i: (b * nq + i, h)),
        out_shape=jax.ShapeDtypeStruct((n_batch * n, w), BF16),
        compiler_params=_params(("parallel", "parallel", "arbitrary"), vmem),
        name="diff_latent",
    )(u, u, u, cache_k, cache_v, lam_p, subln_g.reshape(1, dv))


def kernel(x_prompt, x_sample, c, cache_a_k, cache_a_v, cache_c_k, cache_c_v, c_ctx, w_mod, b_mod, norm_g,
           ffn_w1, ffn_w3, ffn_w2, a_w_in, a_w_out, a_rpb, b_dw_w, b_dw_b, b_ln_g, b_ln_b, c_w_in, c_w_out,
           c_lambda, c_subln_g, final_g):
    bp, seq, d = x_prompt.shape
    bs, n, _ = x_sample.shape
    depth = w_mod.shape[0]
    assert 1 + bs <= N_COND and n % TM_FFN == 0 and (bp * seq) % TM_FFN == 0 and TM_FFN % TM == 0

    xp = x_prompt.reshape(bp * seq, d)
    xs = x_sample.reshape(bs * n, d)
    conds = jnp.concatenate([c_ctx[None], c, jnp.zeros((N_COND - 1 - bs, d), F32)], axis=0)
    m = _modulation(conds, w_mod, b_mod)

    tiles_per_latent = n // TM
    cond_p = lambda i: 0
    cond_s = lambda i: 1 + i // tiles_per_latent

    w1 = ffn_w1.astype(BF16)
    w3 = ffn_w3.astype(BF16)
    w2 = ffn_w2.astype(BF16)
    a_in, a_out = a_w_in.astype(BF16), a_w_out.astype(BF16)
    c_in, c_out = c_w_in.astype(BF16), c_w_out.astype(BF16)
    ng = norm_g.reshape(depth, N_SUB, 1, d)
    rope = _rope_tables(n)

    new_a_k, new_a_v, new_c_k, new_c_v = [], [], [], []
    for l in range(depth):
        xp = _ffn(xp, m, ng, l, cond_p, w1, w3, w2, 0, 0)
        xs = _ffn(xs, m, ng, l, cond_s, w1, w3, w2, 0, 0)
        if l % 2 == 0:
            e = l // 2
            wa = H_A * DH_A
            cb = b_dw_w.shape[2]
            conv_p = (b_dw_w[e], b_dw_b[e], b_ln_g[e], b_ln_b[e])
            qs = DH_A ** -0.5 * LOG2E
            up, k_heads, v_heads = _proj(xp, m, ng, l, cond_p, a_in, e, 1, F32, seq, wa, qs,
                                         head_outs=[(wa, H_A, DH_A), (2 * wa, H_A, DH_A)])
            us, = _proj(xs, m, ng, l, cond_s, a_in, e, 1, BF16, TM, wa, qs)
            new_a_k.append(k_heads)
            new_a_v.append(v_heads)
            op = _attn_prompt(up, bp, seq)
            cp = _conv_module(up, 3 * wa, cb, seq, *conv_p)
            os_ = _na_latent(us, cache_a_k, cache_a_v, e, _na_col_table(a_rpb[e]), bs, n)
            cs = _conv_module(us, 3 * wa, cb, n, *conv_p)
            xp = _outproj([op, cp], a_out, e, xp, m, l, cond_p)
            xs = _outproj([os_, cs], a_out, e, xs, m, l, cond_s)
        else:
            o = l // 2
            lam_init = 0.8 - 0.6 * math.exp(-0.3 * l)
            wqk = 2 * H_C * DH_C
            qs = DH_C ** -0.5 * LOG2E
            up, k_heads, v_heads = _proj(xp, m, ng, l, cond_p, c_in, o, 1, F32, seq, wqk, qs,
                                         head_outs=[(wqk, 2 * H_C, DH_C), (2 * wqk, H_C, 2 * DH_C)])
            us, = _proj(xs, m, ng, l, cond_s, c_in, o, 1, BF16, TM, wqk, qs, rope=rope,
                        n_rope_cols=2 * wqk)
            new_c_k.append(k_heads)
            new_c_v.append(v_heads)
            op = _diff_prompt(up, c_lambda[o], c_subln_g[o], lam_init, bp, seq)
            os_ = _diff_latent(us, cache_c_k, cache_c_v, o, c_lambda[o], c_subln_g[o], lam_init, bs, n)
            xp = _outproj([op], c_out, o, xp, m, l, cond_p)
            xs = _outproj([os_], c_out, o, xs, m, l, cond_s)
        fg = final_g if l == depth - 1 else None
        xp = _ffn(xp, m, ng, l, cond_p, w1, w3, w2, 1, 2, final_g=fg)
        xs = _ffn(xs, m, ng, l, cond_s, w1, w3, w2, 1, 2, final_g=fg)

    return (xp.reshape(bp, seq, d), xs.reshape(bs, n, d),
            jnp.stack(new_a_k, axis=1), jnp.stack(new_a_v, axis=1),
            jnp.stack(new_c_k, axis=1), jnp.stack(new_c_v, axis=1))
```

```python
import functools
import math

import jax
import jax.numpy as jnp
import numpy as np
from jax import lax
from jax.experimental import pallas as pl
from jax.experimental.pallas import tpu as pltpu

F32 = jnp.float32
BF16 = jnp.bfloat16

GRID_W = 64
H_A = 8
DH_A = 128
WIN_H = 8
WIN_W = 16
CONV_W = 31
H_C = 8
DH_C = 128
N_SUB = 3
N_MOD = 3 * N_SUB
EPS = 1e-6
ROPE_BASE = 10000.0
LOG2E = 1.0 / math.log(2.0)

LANES = 128
SUBLANES = 8
BF16_ROWS = 16
VMEM_CAP_BYTES = 64 * 1024 * 1024
NEG_BIG = -1e30

N_COND = 8
TM = 512
TM_FFN = 1024
TF = 512
TN_PROJ = 1024
TN_MOD = 1024
CONV_TM = 128
CONV_HALO = 16
NA_QROWS = 8
NA_KROWS = 16
NA_HP = 2
DIFF_TQ = 512
DIFF_KC = 1024
SM_RB = 32


def _params(sem, vmem_bytes):
    limit = min(int(vmem_bytes * 1.25) + (4 << 20), VMEM_CAP_BYTES - (6 << 20))
    return pltpu.CompilerParams(dimension_semantics=sem, vmem_limit_bytes=limit)


def _sigmoid(x):
    return 1.0 / (1.0 + jnp.exp(-x))


def _dot(a, b):
    return jnp.dot(a, b, preferred_element_type=F32)


def _dot_nt(a, b):
    return lax.dot_general(a, b, (((1,), (1,)), ((), ())), preferred_element_type=F32)


def _adaln(x, m_ref, g_ref, sub):
    r = lax.rsqrt(jnp.mean(x * x, axis=-1, keepdims=True) + EPS)
    a = g_ref[...] * (1.0 + m_ref[3 * sub + 1:3 * sub + 2, :])
    return x * r * a + m_ref[3 * sub:3 * sub + 1, :]


def _row_splits(tm, n_parts):
    cuts = [round(k * tm / n_parts / BF16_ROWS) * BF16_ROWS for k in range(n_parts + 1)]
    return list(zip(cuts[:-1], cuts[1:]))


def _mod_kernel(c_ref, w_ref, b_ref, o_ref):
    c = c_ref[...]
    s = (c * _sigmoid(c)).astype(BF16)
    o_ref[...] = _dot(s, w_ref[...].astype(BF16)) + b_ref[...]


def _modulation(conds, w_mod, b_mod):
    depth, d, n = w_mod.shape
    tn = TN_MOD
    vmem = 2 * d * tn * 4 + d * tn * 2 + 4 * N_COND * tn * 4
    out = pl.pallas_call(
        _mod_kernel,
        grid=(depth, n // tn),
        in_specs=[
            pl.BlockSpec((N_COND, d), lambda l, j: (0, 0)),
            pl.BlockSpec((None, d, tn), lambda l, j: (l, 0, j)),
            pl.BlockSpec((None, 1, tn), lambda l, j: (l, 0, j)),
        ],
        out_specs=pl.BlockSpec((None, N_COND, tn), lambda l, j: (l, 0, j)),
        out_shape=jax.ShapeDtypeStruct((depth, N_COND, n), F32),
        compiler_params=_params(("arbitrary", "arbitrary"), vmem),
        name="modulation",
    )(conds, w_mod, b_mod.reshape(depth, 1, n))
    return out.reshape(depth, N_COND, N_MOD, d)


def _ffn_kernel(x_ref, m_ref, g_ref, w1_ref, w3_ref, w2_ref, *rest, sub, final):
    if final:
        fg_ref, o_ref, h_scr = rest
    else:
        o_ref, h_scr = rest
    j = pl.program_id(1)

    def dff_tile():
        h = h_scr[...]
        a = _dot(h, w1_ref[...])
        b = _dot(h, w3_ref[...])
        return _dot((a * _sigmoid(a) * b).astype(BF16), w2_ref[...])

    @pl.when(j == 0)
    def _():
        h_scr[...] = _adaln(x_ref[...], m_ref, g_ref, sub).astype(BF16)
        o_ref[...] = dff_tile()

    @pl.when(j > 0)
    def _():
        o_ref[...] += dff_tile()

    @pl.when(j == pl.num_programs(1) - 1)
    def _():
        y = x_ref[...] + (0.5 * m_ref[3 * sub + 2:3 * sub + 3, :]) * o_ref[...]
        if final:
            y = y * lax.rsqrt(jnp.mean(y * y, axis=-1, keepdims=True) + EPS) * fg_ref[...]
        o_ref[...] = y


def _ffn(x, m, norm_g, layer, cond_of_tile, w1, w3, w2, which, sub, final_g=None):
    t, d = x.shape
    f = w1.shape[-1]
    tm = TM_FFN
    final = final_g is not None
    in_specs = [
        pl.BlockSpec((tm, d), lambda i, j: (i, 0)),
        pl.BlockSpec((None, None, N_MOD, d), lambda i, j: (layer, cond_of_tile(i * (tm // TM)), 0, 0)),
        pl.BlockSpec((None, None, 1, d), lambda i, j: (layer, sub, 0, 0)),
        pl.BlockSpec((None, None, d, TF), lambda i, j: (layer, which, 0, j)),
        pl.BlockSpec((None, None, d, TF), lambda i, j: (layer, which, 0, j)),
        pl.BlockSpec((None, None, TF, d), lambda i, j: (layer, which, j, 0)),
    ]
    args = [x, m, norm_g, w1, w3, w2]
    if final:
        in_specs.append(pl.BlockSpec((1, d), lambda i, j: (0, 0)))
        args.append(final_g.reshape(1, d))
    vmem = 4 * tm * d * 4 + tm * d * 2 + 2 * 3 * d * TF * 2 + 3 * tm * TF * 4
    return pl.pallas_call(
        functools.partial(_ffn_kernel, sub=sub, final=final),
        grid=(t // tm, f // TF),
        in_specs=in_specs,
        out_specs=pl.BlockSpec((tm, d), lambda i, j: (i, 0)),
        out_shape=jax.ShapeDtypeStruct((t, d), F32),
        scratch_shapes=[pltpu.VMEM((tm, d), BF16)],
        compiler_params=_params(("parallel", "arbitrary"), vmem),
        name="ffn",
    )(*args)


def _rope(u, cos, sin, first_half):
    partner = jnp.where(first_half, pltpu.roll(u, 96, 1), pltpu.roll(u, 32, 1))
    return u * cos + partner * sin


def _proj_kernel(x_ref, xn_ref, m_ref, mn_ref, g_ref, w_ref, *rest, sub, tn, n_q_tiles, q_scale,
                 n_rope_tiles, head_outs):
    if n_rope_tiles:
        cos_ref, sin_ref = rest[:2]
        rest = rest[2:]
        cos, sin = cos_ref[...], sin_ref[...]
        lane = lax.broadcasted_iota(jnp.int32, cos.shape, 1)
        first_half = (lane % (DH_C // 2)) < (DH_C // 4)
    o_ref, head_refs, h_scr = rest[0], rest[1:-1], rest[-1]
    i = pl.program_id(0)
    slot = i % 2

    @pl.when(i == 0)
    def _():
        h_scr[0] = _adaln(x_ref[...], m_ref, g_ref, sub).astype(BF16)

    n_tiles = o_ref.shape[1] // tn
    for j, (r0, r1) in enumerate(_row_splits(o_ref.shape[0], n_tiles)):
        u = _dot(h_scr[slot], w_ref[:, j * tn:(j + 1) * tn])
        if j < n_q_tiles:
            u = u * q_scale
        h_scr[1 - slot, r0:r1, :] = _adaln(xn_ref[r0:r1, :], mn_ref, g_ref, sub).astype(BF16)
        if j < n_rope_tiles:
            for hh in range(tn // DH_C):
                sl = slice(hh * DH_C, (hh + 1) * DH_C)
                o_ref[:, j * tn + hh * DH_C:j * tn + (hh + 1) * DH_C] = _rope(
                    u[:, sl], cos, sin, first_half).astype(o_ref.dtype)
        else:
            o_ref[:, j * tn:(j + 1) * tn] = u.astype(o_ref.dtype)
        for ref, (col0, n_heads, dh) in zip(head_refs, head_outs):
            for hh in range(n_heads):
                c0 = col0 + hh * dh - j * tn
                if 0 <= c0 < tn:
                    ref[hh] = u[:, c0:c0 + dh]


def _proj(x, m, norm_g, layer, cond_of_tile, w, widx, sub, out_dtype, tm, n_q_cols, q_scale, rope=None,
          n_rope_cols=0, head_outs=()):
    t, d = x.shape
    n = w.shape[-1]
    tn = TN_PROJ
    nt = t // tm
    assert n % tn == 0 and n_q_cols % tn == 0 and n_rope_cols % tn == 0 and TM % tm == 0
    assert all(c0 >= n_q_cols and c0 % dh == 0 and tn % dh == 0 for c0, _, dh in head_outs)
    nxt = lambda i: jnp.minimum(i + 1, nt - 1)
    cond = lambda i: cond_of_tile(i * tm // TM)
    once = pl.Buffered(1)
    in_specs = [
        pl.BlockSpec((tm, d), lambda i: (0, 0), pipeline_mode=once),
        pl.BlockSpec((tm, d), lambda i: (nxt(i), 0)),
        pl.BlockSpec((None, None, N_MOD, d), lambda i: (layer, cond(0), 0, 0)),
        pl.BlockSpec((None, None, N_MOD, d), lambda i: (layer, cond(nxt(i)), 0, 0)),
        pl.BlockSpec((None, None, 1, d), lambda i: (layer, sub, 0, 0)),
        pl.BlockSpec((None, d, n), lambda i: (widx, 0, 0), pipeline_mode=once),
    ]
    args = [x, x, m, m, norm_g, w]
    if rope is not None:
        cos, sin = rope
        tiles_per_seq = cos.shape[0] // tm
        in_specs += [pl.BlockSpec((tm, DH_C), lambda i: (i % tiles_per_seq, 0))] * 2
        args += [cos, sin]
    out_specs = [pl.BlockSpec((tm, n), lambda i: (i, 0))]
    out_shape = [jax.ShapeDtypeStruct((t, n), out_dtype)]
    for _, n_heads, dh in head_outs:
        out_specs.append(pl.BlockSpec((None, n_heads, tm, dh), lambda i: (i, 0, 0, 0)))
        out_shape.append(jax.ShapeDtypeStruct((nt, n_heads, tm, dh), F32))
    osz = jnp.dtype(out_dtype).itemsize
    vmem = (3 * tm * d * 4 + 2 * tm * d * 2 + d * n * 2 + 2 * tm * n * osz + 4 * tm * tn * 4
            + sum(2 * n_heads * tm * dh * 4 for _, n_heads, dh in head_outs))
    return pl.pallas_call(
        functools.partial(_proj_kernel, sub=sub, tn=tn, n_q_tiles=n_q_cols // tn, q_scale=q_scale,
                          n_rope_tiles=n_rope_cols // tn, head_outs=tuple(head_outs)),
        grid=(nt,),
        in_specs=in_specs,
        out_specs=out_specs,
        out_shape=out_shape,
        scratch_shapes=[pltpu.VMEM((2, tm, d), BF16)],
        compiler_params=_params(("arbitrary",), vmem),
        name="in_proj",
    )(*args)


def _rope_tables(n):
    t = jnp.arange(n)
    nf = DH_C // 4
    inv = ROPE_BASE ** (-jnp.arange(nf, dtype=F32) / nf)
    ang_r = (t // GRID_W).astype(F32)[:, None] * inv[None, :]
    ang_c = (t % GRID_W).astype(F32)[:, None] * inv[None, :]
    cos = jnp.concatenate([jnp.cos(ang_r)] * 2 + [jnp.cos(ang_c)] * 2, axis=-1)
    sin = jnp.concatenate([-jnp.sin(ang_r), jnp.sin(ang_r), -jnp.sin(ang_c), jnp.sin(ang_c)], axis=-1)
    return cos, sin


def _outproj_kernel(*refs, ks):
    ins = refs[:len(ks)]
    w_ref, x_ref, m_ref, o_ref = refs[len(ks):]
    y = None
    off = 0
    for r, k in zip(ins, ks):
        part = _dot(r[...], w_ref[off:off + k, :])
        y = part if y is None else y + part
        off += k
    o_ref[...] = x_ref[...] + m_ref[5:6, :] * y


def _outproj(ins, w, widx, x, m, layer, cond_of_tile):
    t, d = x.shape
    ks = tuple(a.shape[1] for a in ins)
    kin = sum(ks)
    in_specs = [pl.BlockSpec((TM, k), lambda i: (i, 0)) for k in ks] + [
        pl.BlockSpec((None, kin, d), lambda i: (widx, 0, 0)),
        pl.BlockSpec((TM, d), lambda i: (i, 0)),
        pl.BlockSpec((None, None, N_MOD, d), lambda i: (layer, cond_of_tile(i), 0, 0)),
    ]
    vmem = 2 * kin * d * 2 + 2 * TM * kin * 2 + 4 * TM * d * 4 + 2 * TM * d * 4
    return pl.pallas_call(
        functools.partial(_outproj_kernel, ks=ks),
        grid=(t // TM,),
        in_specs=in_specs,
        out_specs=pl.BlockSpec((TM, d), lambda i: (i, 0)),
        out_shape=jax.ShapeDtypeStruct((t, d), F32),
        compiler_params=_params(("parallel",), vmem),
        name="out_proj",
    )(*ins, w, x, m)


def _attn_p_kernel(q_ref, k_ref, v_ref, o_ref, *, n_heads, dh):
    for h in range(n_heads):
        sl = slice(h * dh, (h + 1) * dh)
        s = _dot_nt(q_ref[:, sl].astype(BF16), k_ref[:, sl].astype(BF16))
        e = jnp.exp2(s - jnp.max(s, axis=-1, keepdims=True))
        r = 1.0 / jnp.sum(e, axis=-1, keepdims=True)
        o = _dot(e.astype(BF16), v_ref[:, sl].astype(BF16)) * r
        o_ref[:, sl] = o.astype(o_ref.dtype)


def _attn_prompt(u, n_batch, seq):
    w = H_A * DH_A
    vmem = 2 * 3 * seq * w * 4 + 2 * seq * w * 2 + 8 * seq * seq * 4
    return pl.pallas_call(
        functools.partial(_attn_p_kernel, n_heads=H_A, dh=DH_A),
        grid=(n_batch,),
        in_specs=[pl.BlockSpec((seq, w), lambda b: (b, 0)),
                  pl.BlockSpec((seq, w), lambda b: (b, 1)),
                  pl.BlockSpec((seq, w), lambda b: (b, 2))],
        out_specs=pl.BlockSpec((seq, w), lambda b: (b, 0)),
        out_shape=jax.ShapeDtypeStruct((n_batch * seq, w), BF16),
        compiler_params=_params(("parallel",), vmem),
        name="attn_prompt",
    )(u, u, u)


def _na_col_table(rpb):
    h = rpb.shape[0]
    rpb2 = rpb.astype(F32) * LOG2E
    period = GRID_W + 2 * WIN_W - 1
    rp = jnp.pad(rpb2, ((0, 0), (0, 0), (0, GRID_W)), constant_values=NEG_BIG)
    skew = jnp.tile(rp, (1, 1, GRID_W))[:, :, :GRID_W * (period - 1)].reshape(h, -1, GRID_W, period - 1)
    skew = skew[:, :, :, WIN_W - 1:WIN_W - 1 + GRID_W]
    c = np.arange(GRID_W)
    cs = np.clip(c - WIN_W // 2, 0, GRID_W - WIN_W)
    col_ok = (c[None, :] >= cs[:, None]) & (c[None, :] < cs[:, None] + WIN_W)
    tcol = jnp.where(col_ok, skew, NEG_BIG)
    return jnp.concatenate([tcol, tcol], axis=-1)


def _na_block_types(rows):
    return [(r0, min(max(r0 - (NA_KROWS - NA_QROWS) // 2, 0), rows - NA_KROWS))
            for r0 in (0, NA_QROWS, rows - NA_QROWS)]


def _na_build_bias(tc_ref, bias_scr, rows):
    kh = min(WIN_H, rows)
    lane = lax.broadcasted_iota(jnp.int32, (GRID_W, LANES), 1)
    left = lane < GRID_W
    neg = jnp.full((GRID_W, LANES), NEG_BIG, F32)
    for ty, (r0, base) in enumerate(_na_block_types(rows)):
        for rr in range(NA_QROWS):
            r = r0 + rr
            rs = min(max(r - kh // 2, 0), rows - kh)
            for p in range(NA_KROWS * GRID_W // LANES):
                halves = []
                for kr in (base + 2 * p, base + 2 * p + 1):
                    halves.append(tc_ref[kr - r + WIN_H - 1] if rs <= kr < rs + kh else None)
                if halves[0] is None and halves[1] is None:
                    tile = neg
                else:
                    tile = jnp.where(left, neg if halves[0] is None else halves[0],
                                     neg if halves[1] is None else halves[1])
                bias_scr[ty, rr * GRID_W:(rr + 1) * GRID_W, p * LANES:(p + 1) * LANES] = tile


def _na_kernel(q_ref, k_ref, v_ref, ck_ref, cv_ref, tc_ref, o_ref, ckb_scr, cvb_scr, bias_scr, ec_scr,
               el_scr, r_scr, *, rows):
    j = pl.program_id(2)
    nj = pl.num_programs(2)

    @pl.when(j == 0)
    def _():
        ckb_scr[...] = ck_ref[...].astype(BF16)
        cvb_scr[...] = cv_ref[...].astype(BF16)
        for hh in range(NA_HP):
            _na_build_bias(tc_ref.at[hh], bias_scr.at[hh], rows)

    ty = (j > 0).astype(jnp.int32) + (j == nj - 1).astype(jnp.int32)
    nk = NA_KROWS * GRID_W
    start = jnp.clip(j * NA_QROWS - (NA_KROWS - NA_QROWS) // 2, 0, rows - NA_KROWS) * GRID_W
    start = pl.multiple_of(start, 256)
    for hh in range(NA_HP):
        sl = slice(hh * DH_A, (hh + 1) * DH_A)
        bias_ref = bias_scr.at[hh, ty]
        q = q_ref[:, sl]
        s_c = _dot_nt(q, ckb_scr[hh])
        s_l = _dot_nt(q, k_ref[pl.ds(start, nk), sl])
        for rb in range(q.shape[0] // SM_RB):
            rws = slice(rb * SM_RB, (rb + 1) * SM_RB)
            sc = s_c[rws]
            sk = s_l[rws] + bias_ref[rws, :]
            mx = jnp.maximum(jnp.max(sc, axis=-1, keepdims=True), jnp.max(sk, axis=-1, keepdims=True))
            e_c = jnp.exp2(sc - mx)
            e_l = jnp.exp2(sk - mx)
            r_scr[hh, rws, :] = 1.0 / (jnp.sum(e_c, axis=-1, keepdims=True)
                                       + jnp.sum(e_l, axis=-1, keepdims=True))
            ec_scr[hh, rws, :] = e_c.astype(BF16)
            el_scr[hh, rws, :] = e_l.astype(BF16)
        o = _dot(ec_scr[hh], cvb_scr[hh]) + _dot(el_scr[hh], v_ref[pl.ds(start, nk), sl])
        o_ref[:, sl] = (o * r_scr[hh]).astype(o_ref.dtype)


def _na_latent(u, cache_k, cache_v, e, col_table, n_batch, n):
    rows = n // GRID_W
    nj = rows // NA_QROWS
    tq = NA_QROWS * GRID_W
    nk = NA_KROWS * GRID_W
    ctx = cache_k.shape[3]
    w = H_A * DH_A
    n_rel = col_table.shape[1]
    hp = NA_HP
    ng = H_A // hp
    assert 2 * GRID_W == LANES and nj >= 3 and H_A % hp == 0
    vmem = hp * (2 * tq * DH_A * 2 * 2 + 2 * 2 * n * DH_A * 2 + 2 * 2 * ctx * DH_A * 4 + 3 * tq * nk * 4
                 + 2 * n_rel * GRID_W * LANES * 4 + 2 * ctx * DH_A * 2 + tq * (nk + ctx) * 2
                 ) + 3 * tq * (nk + ctx) * 4
    return pl.pallas_call(
        functools.partial(_na_kernel, rows=rows),
        grid=(n_batch, ng, nj),
        in_specs=[
            pl.BlockSpec((tq, hp * DH_A), lambda b, h, j: (b * nj + j, h)),
            pl.BlockSpec((n, hp * DH_A), lambda b, h, j: (b, ng + h)),
            pl.BlockSpec((n, hp * DH_A), lambda b, h, j: (b, 2 * ng + h)),
            pl.BlockSpec((None, None, hp, ctx, DH_A), lambda b, h, j: (b, e, h, 0, 0)),
            pl.BlockSpec((None, None, hp, ctx, DH_A), lambda b, h, j: (b, e, h, 0, 0)),
            pl.BlockSpec((hp, n_rel, GRID_W, LANES), lambda b, h, j: (h, 0, 0, 0)),
        ],
        out_specs=pl.BlockSpec((tq, hp * DH_A), lambda b, h, j: (b * nj + j, h)),
        out_shape=jax.ShapeDtypeStruct((n_batch * n, w), BF16),
        scratch_shapes=[pltpu.VMEM((hp, ctx, DH_A), BF16), pltpu.VMEM((hp, ctx, DH_A), BF16),
                        pltpu.VMEM((hp, 3, tq, nk), F32),
                        pltpu.VMEM((hp, tq, ctx), BF16), pltpu.VMEM((hp, tq, nk), BF16),
                        pltpu.VMEM((hp, tq, 1), F32)],
        compiler_params=_params(("parallel", "parallel", "arbitrary"), vmem),
        name="na_latent",
    )(u, u, u, cache_k, cache_v, col_table)


def _conv_kernel(a_ref, g_ref, pa_ref, pg_ref, na_ref, ng_ref, w_ref, b_ref, lg_ref, lb_ref, o_ref,
                 z_scr, y_scr, zs_scr, *, tiles_per_seq):
    tm, c = o_ref.shape
    t = pl.program_id(0) % tiles_per_seq

    def glu(a, g):
        return a.astype(F32) * _sigmoid(g.astype(F32))

    z_scr[0:CONV_HALO, :] = jnp.where(t == 0, 0.0, glu(pa_ref[...], pg_ref[...]))
    z_scr[CONV_HALO:CONV_HALO + tm, :] = glu(a_ref[...], g_ref[...])
    z_scr[CONV_HALO + tm:, :] = jnp.where(t == tiles_per_seq - 1, 0.0, glu(na_ref[...], ng_ref[...]))

    first = CONV_HALO - CONV_W // 2
    win = tm + 2 * CONV_HALO - SUBLANES
    for cb in range(c // LANES):
        cols = slice(cb * LANES, (cb + 1) * LANES)
        zc = z_scr[:, cols]
        zs = zs_scr.at[cb % 2]
        for b in range(1, SUBLANES):
            zs[b] = zc[b:b + win]
        acc = jnp.zeros((tm, LANES), F32)
        for b in range(SUBLANES):
            for a8 in range(2 * CONV_HALO // SUBLANES):
                k = SUBLANES * a8 + b - first
                if 0 <= k < CONV_W:
                    rows = slice(SUBLANES * a8, SUBLANES * a8 + tm)
                    zb = z_scr[rows, cols] if b == 0 else zs[b, rows, :]
                    acc = acc + w_ref[k:k + 1, cols] * zb
        y_scr[:, cols] = acc + b_ref[:, cols]

    y = y_scr[...]
    mu = jnp.mean(y, axis=-1, keepdims=True)
    yc = y - mu
    yn = yc * lax.rsqrt(jnp.mean(yc * yc, axis=-1, keepdims=True) + EPS) * lg_ref[...] + lb_ref[...]
    o_ref[...] = (yn * _sigmoid(yn)).astype(o_ref.dtype)


def _conv_module(u, col0, c, seq, dw_w, dw_b, ln_g, ln_b):
    t = u.shape[0]
    tm = CONV_TM
    tiles_per_seq = seq // tm
    hb = tm // CONV_HALO
    n_halo = t // CONV_HALO
    ca, cg = col0 // c, col0 // c + 1
    isz = u.dtype.itemsize
    in_specs = [
        pl.BlockSpec((tm, c), lambda i: (i, ca)),
        pl.BlockSpec((tm, c), lambda i: (i, cg)),
        pl.BlockSpec((CONV_HALO, c), lambda i: (jnp.maximum(i * hb - 1, 0), ca)),
        pl.BlockSpec((CONV_HALO, c), lambda i: (jnp.maximum(i * hb - 1, 0), cg)),
        pl.BlockSpec((CONV_HALO, c), lambda i: (jnp.minimum((i + 1) * hb, n_halo - 1), ca)),
        pl.BlockSpec((CONV_HALO, c), lambda i: (jnp.minimum((i + 1) * hb, n_halo - 1), cg)),
        pl.BlockSpec((CONV_W, c), lambda i: (0, 0)),
        pl.BlockSpec((1, c), lambda i: (0, 0)),
        pl.BlockSpec((1, c), lambda i: (0, 0)),
        pl.BlockSpec((1, c), lambda i: (0, 0)),
    ]
    vmem = (4 * (tm + 2 * CONV_HALO) * c * isz + 2 * tm * c * 2 + (2 * tm + 2 * CONV_HALO) * c * 4
            + 4 * CONV_W * c * 4 + 6 * tm * c * 4)
    return pl.pallas_call(
        functools.partial(_conv_kernel, tiles_per_seq=tiles_per_seq),
        grid=(t // tm,),
        in_specs=in_specs,
        out_specs=pl.BlockSpec((tm, c), lambda i: (i, 0)),
        out_shape=jax.ShapeDtypeStruct((t, c), BF16),
        scratch_shapes=[pltpu.VMEM((tm + 2 * CONV_HALO, c), F32), pltpu.VMEM((tm, c), F32),
                        pltpu.VMEM((2, SUBLANES, tm + 2 * CONV_HALO - SUBLANES, LANES), F32)],
        compiler_params=_params(("parallel",), vmem),
        name="conv_module",
    )(u, u, u, u, u, u, dw_w, dw_b.reshape(1, c), ln_g.reshape(1, c), ln_b.reshape(1, c))


def _diff_lambda(lam_ref, lam_init):
    lp = lam_ref[...].astype(F32)
    s01 = jnp.sum(lp[0:1] * lp[1:2], axis=-1, keepdims=True)
    s23 = jnp.sum(lp[2:3] * lp[3:4], axis=-1, keepdims=True)
    return jnp.exp(s01) - jnp.exp(s23) + lam_init


def _subln(o, g_ref, lam_init):
    return o * lax.rsqrt(jnp.mean(o * o, axis=-1, keepdims=True) + EPS) * (g_ref[...] * (1.0 - lam_init))


def _diff_p_kernel(q_ref, k_ref, v_ref, lam_ref, g_ref, o_ref, *, lam_init):
    lam = _diff_lambda(lam_ref, lam_init)
    dv = 2 * DH_C
    for h in range(H_C):
        p = []
        for i in range(2):
            sl = slice((2 * h + i) * DH_C, (2 * h + i + 1) * DH_C)
            s = _dot_nt(q_ref[:, sl].astype(BF16), k_ref[:, sl].astype(BF16))
            e = jnp.exp2(s - jnp.max(s, axis=-1, keepdims=True))
            p.append(e * (1.0 / jnp.sum(e, axis=-1, keepdims=True)))
        a = (p[0] - lam * p[1]).astype(BF16)
        vs = slice(h * dv, (h + 1) * dv)
        o = _dot(a, v_ref[:, vs].astype(BF16))
        o_ref[:, vs] = _subln(o, g_ref, lam_init).astype(o_ref.dtype)


def _diff_prompt(u, lam_p, subln_g, lam_init, n_batch, seq):
    w = 2 * H_C * DH_C
    vmem = 2 * 3 * seq * w * 4 + 2 * seq * w * 2 + 10 * seq * seq * 4
    return pl.pallas_call(
        functools.partial(_diff_p_kernel, lam_init=lam_init),
        grid=(n_batch,),
        in_specs=[pl.BlockSpec((seq, w), lambda b: (b, 0)),
                  pl.BlockSpec((seq, w), lambda b: (b, 1)),
                  pl.BlockSpec((seq, w), lambda b: (b, 2)),
                  pl.BlockSpec((4, DH_C), lambda b: (0, 0)),
                  pl.BlockSpec((1, 2 * DH_C), lambda b: (0, 0))],
        out_specs=pl.BlockSpec((seq, w), lambda b: (b, 0)),
        out_shape=jax.ShapeDtypeStruct((n_batch * seq, w), BF16),
        compiler_params=_params(("parallel",), vmem),
        name="diff_prompt",
    )(u, u, u, lam_p, subln_g.reshape(1, 2 * DH_C))


def _diff_s_kernel(q_ref, k_ref, v_ref, ck_ref, cv_ref, lam_ref, g_ref, o_ref, ckb_scr, cvb_scr, e_scr,
                   acc_scr, *, lam_init):
    @pl.when(pl.program_id(2) == 0)
    def _():
        ckb_scr[...] = ck_ref[...].astype(BF16)
        cvb_scr[...] = cv_ref[...].astype(BF16)

    tq = q_ref.shape[0]
    n_chunks = k_ref.shape[0] // DIFF_KC
    n_rb = tq // SM_RB
    for i in range(2):
        sl = slice(i * DH_C, (i + 1) * DH_C)
        q = q_ref[:, sl]
        mx = [None] * n_rb
        den = [None] * n_rb
        for c in range(n_chunks + 1):
            if c == 0:
                kc, vc = ckb_scr[i], cvb_scr[...]
            else:
                krows = slice((c - 1) * DIFF_KC, c * DIFF_KC)
                kc, vc = k_ref[krows, sl], v_ref[krows, :]
            nkc = kc.shape[0]
            s = _dot_nt(q, kc)
            alpha = [None] * n_rb
            for rb in range(n_rb):
                rws = slice(rb * SM_RB, (rb + 1) * SM_RB)
                sb = s[rws]
                cmax = jnp.max(sb, axis=-1, keepdims=True)
                if c == 0:
                    mx[rb] = cmax
                    e = jnp.exp2(sb - cmax)
                    den[rb] = jnp.sum(e, axis=-1, keepdims=True)
                else:
                    m_new = jnp.maximum(mx[rb], cmax)
                    alpha[rb] = jnp.exp2(mx[rb] - m_new)
                    e = jnp.exp2(sb - m_new)
                    den[rb] = den[rb] * alpha[rb] + jnp.sum(e, axis=-1, keepdims=True)
                    mx[rb] = m_new
                e_scr[rws, :nkc] = e.astype(BF16)
            pv = _dot(e_scr[:, :nkc], vc)
            for rb in range(n_rb):
                rws = slice(rb * SM_RB, (rb + 1) * SM_RB)
                if c == 0:
                    acc_scr[i, rws, :] = pv[rws]
                else:
                    acc_scr[i, rws, :] = acc_scr[i, rws, :] * alpha[rb] + pv[rws]
        for rb in range(n_rb):
            rws = slice(rb * SM_RB, (rb + 1) * SM_RB)
            acc_scr[i, rws, :] = acc_scr[i, rws, :] * (1.0 / den[rb])
    o = acc_scr[0] - _diff_lambda(lam_ref, lam_init) * acc_scr[1]
    o_ref[...] = _subln(o, g_ref, lam_init).astype(o_ref.dtype)


def _diff_latent(u, cache_k, cache_v, o_idx, lam_p, subln_g, lam_init, n_batch, n):
    tq = DIFF_TQ
    nq = n // tq
    ctx = cache_k.shape[3]
    dv = 2 * DH_C
    w = H_C * dv
    nkh = w // dv
    kmax = max(DIFF_KC, ctx)
    vmem = (2 * tq * dv * 2 * 2 + 2 * 2 * n * dv * 2 + 2 * 2 * ctx * dv * 4 + 2 * ctx * dv * 2
            + tq * kmax * 2 + 2 * tq * dv * 4 + 3 * tq * kmax * 4 + 2 * tq * dv * 4)
    return pl.pallas_call(
        functools.partial(_diff_s_kernel, lam_init=lam_init),
        scratch_shapes=[pltpu.VMEM((2, ctx, DH_C), BF16), pltpu.VMEM((ctx, dv), BF16),
                        pltpu.VMEM((tq, kmax), BF16), pltpu.VMEM((2, tq, dv), F32)],
        grid=(n_batch, H_C, nq),
        in_specs=[
            pl.BlockSpec((tq, dv), lambda b, h, i: (b * nq + i, h)),
            pl.BlockSpec((n, dv), lambda b, h, i: (b, nkh + h)),
            pl.BlockSpec((n, dv), lambda b, h, i: (b, 2 * nkh + h)),
            pl.BlockSpec((None, None, 2, ctx, DH_C), lambda b, h, i: (b, o_idx, h, 0, 0)),
            pl.BlockSpec((None, None, None, ctx, dv), lambda b, h, i: (b, o_idx, h, 0, 0)),
            pl.BlockSpec((4, DH_C), lambda b, h, i: (0, 0)),
            pl.BlockSpec((1, dv), lambda b, h, i: (0, 0)),
        ],
        out_specs=pl.BlockSpec((tq, dv), lambda b, h, i: (b * nq + i, h)),
        out_shape=jax.ShapeDtypeStruct((n_batch * n, w), BF16),
        compiler_params=_params(("parallel", "parallel", "arbitrary"), vmem),
        name="diff_latent",
    )(u, u, u, cache_k, cache_v, lam_p, subln_g.reshape(1, dv))


def kernel(x_prompt, x_sample, c, cache_a_k, cache_a_v, cache_c_k, cache_c_v, c_ctx, w_mod, b_mod, norm_g,
           ffn_w1, ffn_w3, ffn_w2, a_w_in, a_w_out, a_rpb, b_dw_w, b_dw_b, b_ln_g, b_ln_b, c_w_in, c_w_out,
           c_lambda, c_subln_g, final_g):
    bp, seq, d = x_prompt.shape
    bs, n, _ = x_sample.shape
    depth = w_mod.shape[0]
    assert 1 + bs <= N_COND and n % TM_FFN == 0 and (bp * seq) % TM_FFN == 0 and TM_FFN % TM == 0

    xp = x_prompt.reshape(bp * seq, d)
    xs = x_sample.reshape(bs * n, d)
    conds = jnp.concatenate([c_ctx[None], c, jnp.zeros((N_COND - 1 - bs, d), F32)], axis=0)
    m = _modulation(conds, w_mod, b_mod)

    tiles_per_latent = n // TM
    cond_p = lambda i: 0
    cond_s = lambda i: 1 + i // tiles_per_latent

    w1 = ffn_w1.astype(BF16)
    w3 = ffn_w3.astype(BF16)
    w2 = ffn_w2.astype(BF16)
    a_in, a_out = a_w_in.astype(BF16), a_w_out.astype(BF16)
    c_in, c_out = c_w_in.astype(BF16), c_w_out.astype(BF16)
    ng = norm_g.reshape(depth, N_SUB, 1, d)
    rope = _rope_tables(n)

    new_a_k, new_a_v, new_c_k, new_c_v = [], [], [], []
    for l in range(depth):
        xp = _ffn(xp, m, ng, l, cond_p, w1, w3, w2, 0, 0)
        xs = _ffn(xs, m, ng, l, cond_s, w1, w3, w2, 0, 0)
        if l % 2 == 0:
            e = l // 2
            wa = H_A * DH_A
            cb = b_dw_w.shape[2]
            conv_p = (b_dw_w[e], b_dw_b[e], b_ln_g[e], b_ln_b[e])
            qs = DH_A ** -0.5 * LOG2E
            up, k_heads, v_heads = _proj(xp, m, ng, l, cond_p, a_in, e, 1, F32, seq, wa, qs,
                                         head_outs=[(wa, H_A, DH_A), (2 * wa, H_A, DH_A)])
            us, = _proj(xs, m, ng, l, cond_s, a_in, e, 1, BF16, TM, wa, qs)
            new_a_k.append(k_heads)
            new_a_v.append(v_heads)
            op = _attn_prompt(up, bp, seq)
            cp = _conv_module(up, 3 * wa, cb, seq, *conv_p)
            os_ = _na_latent(us, cache_a_k, cache_a_v, e, _na_col_table(a_rpb[e]), bs, n)
            cs = _conv_module(us, 3 * wa, cb, n, *conv_p)
            xp = _outproj([op, cp], a_out, e, xp, m, l, cond_p)
            xs = _outproj([os_, cs], a_out, e, xs, m, l, cond_s)
        else:
            o = l // 2
            lam_init = 0.8 - 0.6 * math.exp(-0.3 * l)
            wqk = 2 * H_C * DH_C
            qs = DH_C ** -0.5 * LOG2E
            up, k_heads, v_heads = _proj(xp, m, ng, l, cond_p, c_in, o, 1, F32, seq, wqk, qs,
                                         head_outs=[(wqk, 2 * H_C, DH_C), (2 * wqk, H_C, 2 * DH_C)])
            us, = _proj(xs, m, ng, l, cond_s, c_in, o, 1, BF16, TM, wqk, qs, rope=rope,
                        n_rope_cols=2 * wqk)
            new_c_k.append(k_heads)
            new_c_v.append(v_heads)
            op = _diff_prompt(up, c_lambda[o], c_subln_g[o], lam_init, bp, seq)
            os_ = _diff_latent(us, cache_c_k, cache_c_v, o, c_lambda[o], c_subln_g[o], lam_init, bs, n)
            xp = _outproj([op], c_out, o, xp, m, l, cond_p)
            xs = _outproj([os_], c_out, o, xs, m, l, cond_s)
        fg = final_g if l == depth - 1 else None
        xp = _ffn(xp, m, ng, l, cond_p, w1, w3, w2, 1, 2, final_g=fg)
        xs = _ffn(xs, m, ng, l, cond_s, w1, w3, w2, 1, 2, final_g=fg)

    return (xp.reshape(bp, seq, d), xs.reshape(bs, n, d),
            jnp.stack(new_a_k, axis=1), jnp.stack(new_a_v, axis=1),
            jnp.stack(new_c_k, axis=1), jnp.stack(new_c_v, axis=1))
```

```python
import functools
import math

import jax
import jax.numpy as jnp
import numpy as np
from jax import lax
from jax.experimental import pallas as pl
from jax.experimental.pallas import tpu as pltpu

F32 = jnp.float32
BF16 = jnp.bfloat16

GRID_W = 64
H_A = 8
DH_A = 128
WIN_H = 8
WIN_W = 16
CONV_W = 31
H_C = 8
DH_C = 128
N_SUB = 3
N_MOD = 3 * N_SUB
EPS = 1e-6
ROPE_BASE = 10000.0
LOG2E = 1.0 / math.log(2.0)

LANES = 128
SUBLANES = 8
BF16_ROWS = 16
VMEM_CAP_BYTES = 64 * 1024 * 1024
NEG_BIG = -1e30

N_COND = 8
TM = 512
TM_FFN = 1024
TF = 512
TN_PROJ = 1024
TN_MOD = 1024
CONV_TM = 128
CONV_HALO = 16
NA_QROWS = 8
NA_KROWS = 16
NA_HP = 2
DIFF_TQ = 512
DIFF_KC = 1024
DIFF_HP = 2
SM_RB = 32


def _params(sem, vmem_bytes):
    limit = min(int(vmem_bytes * 1.25) + (4 << 20), VMEM_CAP_BYTES - (6 << 20))
    return pltpu.CompilerParams(dimension_semantics=sem, vmem_limit_bytes=limit)


def _sigmoid(x):
    return 1.0 / (1.0 + jnp.exp(-x))


def _dot(a, b):
    return jnp.dot(a, b, preferred_element_type=F32)


def _dot_nt(a, b):
    return lax.dot_general(a, b, (((1,), (1,)), ((), ())), preferred_element_type=F32)


def _adaln(x, m_ref, g_ref, sub):
    r = lax.rsqrt(jnp.mean(x * x, axis=-1, keepdims=True) + EPS)
    a = g_ref[...] * (1.0 + m_ref[3 * sub + 1:3 * sub + 2, :])
    return x * r * a + m_ref[3 * sub:3 * sub + 1, :]


def _row_splits(tm, n_parts):
    cuts = [round(k * tm / n_parts / BF16_ROWS) * BF16_ROWS for k in range(n_parts + 1)]
    return list(zip(cuts[:-1], cuts[1:]))


def _mod_kernel(c_ref, w_ref, b_ref, o_ref):
    c = c_ref[...]
    s = (c * _sigmoid(c)).astype(BF16)
    o_ref[...] = _dot(s, w_ref[...].astype(BF16)) + b_ref[...]


def _modulation(conds, w_mod, b_mod):
    depth, d, n = w_mod.shape
    tn = TN_MOD
    vmem = 2 * d * tn * 4 + d * tn * 2 + 4 * N_COND * tn * 4
    out = pl.pallas_call(
        _mod_kernel,
        grid=(depth, n // tn),
        in_specs=[
            pl.BlockSpec((N_COND, d), lambda l, j: (0, 0)),
            pl.BlockSpec((None, d, tn), lambda l, j: (l, 0, j)),
            pl.BlockSpec((None, 1, tn), lambda l, j: (l, 0, j)),
        ],
        out_specs=pl.BlockSpec((None, N_COND, tn), lambda l, j: (l, 0, j)),
        out_shape=jax.ShapeDtypeStruct((depth, N_COND, n), F32),
        compiler_params=_params(("arbitrary", "arbitrary"), vmem),
        name="modulation",
    )(conds, w_mod, b_mod.reshape(depth, 1, n))
    return out.reshape(depth, N_COND, N_MOD, d)


def _ffn_kernel(x_ref, m_ref, g_ref, w1_ref, w3_ref, w2_ref, *rest, sub, final):
    if final:
        fg_ref, o_ref, h_scr = rest
    else:
        o_ref, h_scr = rest
    j = pl.program_id(1)

    def dff_tile():
        h = h_scr[...]
        a = _dot(h, w1_ref[...])
        b = _dot(h, w3_ref[...])
        return _dot((a * _sigmoid(a) * b).astype(BF16), w2_ref[...])

    @pl.when(j == 0)
    def _():
        h_scr[...] = _adaln(x_ref[...], m_ref, g_ref, sub).astype(BF16)
        o_ref[...] = dff_tile()

    @pl.when(j > 0)
    def _():
        o_ref[...] += dff_tile()

    @pl.when(j == pl.num_programs(1) - 1)
    def _():
        y = x_ref[...] + (0.5 * m_ref[3 * sub + 2:3 * sub + 3, :]) * o_ref[...]
        if final:
            y = y * lax.rsqrt(jnp.mean(y * y, axis=-1, keepdims=True) + EPS) * fg_ref[...]
        o_ref[...] = y


def _ffn(x, m, norm_g, layer, cond_of_tile, w1, w3, w2, which, sub, final_g=None):
    t, d = x.shape
    f = w1.shape[-1]
    tm = TM_FFN
    final = final_g is not None
    in_specs = [
        pl.BlockSpec((tm, d), lambda i, j: (i, 0)),
        pl.BlockSpec((None, None, N_MOD, d), lambda i, j: (layer, cond_of_tile(i * (tm // TM)), 0, 0)),
        pl.BlockSpec((None, None, 1, d), lambda i, j: (layer, sub, 0, 0)),
        pl.BlockSpec((None, None, d, TF), lambda i, j: (layer, which, 0, j)),
        pl.BlockSpec((None, None, d, TF), lambda i, j: (layer, which, 0, j)),
        pl.BlockSpec((None, None, TF, d), lambda i, j: (layer, which, j, 0)),
    ]
    args = [x, m, norm_g, w1, w3, w2]
    if final:
        in_specs.append(pl.BlockSpec((1, d), lambda i, j: (0, 0)))
        args.append(final_g.reshape(1, d))
    vmem = 4 * tm * d * 4 + tm * d * 2 + 2 * 3 * d * TF * 2 + 3 * tm * TF * 4
    return pl.pallas_call(
        functools.partial(_ffn_kernel, sub=sub, final=final),
        grid=(t // tm, f // TF),
        in_specs=in_specs,
        out_specs=pl.BlockSpec((tm, d), lambda i, j: (i, 0)),
        out_shape=jax.ShapeDtypeStruct((t, d), F32),
        scratch_shapes=[pltpu.VMEM((tm, d), BF16)],
        compiler_params=_params(("parallel", "arbitrary"), vmem),
        name="ffn",
    )(*args)


def _rope(u, cos, sin, first_half):
    partner = jnp.where(first_half, pltpu.roll(u, 96, 1), pltpu.roll(u, 32, 1))
    return u * cos + partner * sin


def _proj_kernel(x_ref, xn_ref, m_ref, mn_ref, g_ref, w_ref, *rest, sub, tn, n_q_tiles, q_scale,
                 n_rope_tiles, head_outs):
    if n_rope_tiles:
        cos_ref, sin_ref = rest[:2]
        rest = rest[2:]
        cos, sin = cos_ref[...], sin_ref[...]
        lane = lax.broadcasted_iota(jnp.int32, cos.shape, 1)
        first_half = (lane % (DH_C // 2)) < (DH_C // 4)
    o_ref, head_refs, h_scr = rest[0], rest[1:-1], rest[-1]
    i = pl.program_id(0)
    slot = i % 2

    @pl.when(i == 0)
    def _():
        h_scr[0] = _adaln(x_ref[...], m_ref, g_ref, sub).astype(BF16)

    n_tiles = o_ref.shape[1] // tn
    for j, (r0, r1) in enumerate(_row_splits(o_ref.shape[0], n_tiles)):
        u = _dot(h_scr[slot], w_ref[:, j * tn:(j + 1) * tn])
        if j < n_q_tiles:
            u = u * q_scale
        h_scr[1 - slot, r0:r1, :] = _adaln(xn_ref[r0:r1, :], mn_ref, g_ref, sub).astype(BF16)
        if j < n_rope_tiles:
            for hh in range(tn // DH_C):
                sl = slice(hh * DH_C, (hh + 1) * DH_C)
                o_ref[:, j * tn + hh * DH_C:j * tn + (hh + 1) * DH_C] = _rope(
                    u[:, sl], cos, sin, first_half).astype(o_ref.dtype)
        else:
            o_ref[:, j * tn:(j + 1) * tn] = u.astype(o_ref.dtype)
        for ref, (col0, n_heads, dh) in zip(head_refs, head_outs):
            for hh in range(n_heads):
                c0 = col0 + hh * dh - j * tn
                if 0 <= c0 < tn:
                    ref[hh] = u[:, c0:c0 + dh]


def _proj(x, m, norm_g, layer, cond_of_tile, w, widx, sub, out_dtype, tm, n_q_cols, q_scale, rope=None,
          n_rope_cols=0, head_outs=()):
    t, d = x.shape
    n = w.shape[-1]
    tn = TN_PROJ
    nt = t // tm
    assert n % tn == 0 and n_q_cols % tn == 0 and n_rope_cols % tn == 0 and TM % tm == 0
    assert all(c0 >= n_q_cols and c0 % dh == 0 and tn % dh == 0 for c0, _, dh in head_outs)
    nxt = lambda i: jnp.minimum(i + 1, nt - 1)
    cond = lambda i: cond_of_tile(i * tm // TM)
    once = pl.Buffered(1)
    in_specs = [
        pl.BlockSpec((tm, d), lambda i: (0, 0), pipeline_mode=once),
        pl.BlockSpec((tm, d), lambda i: (nxt(i), 0)),
        pl.BlockSpec((None, None, N_MOD, d), lambda i: (layer, cond(0), 0, 0)),
        pl.BlockSpec((None, None, N_MOD, d), lambda i: (layer, cond(nxt(i)), 0, 0)),
        pl.BlockSpec((None, None, 1, d), lambda i: (layer, sub, 0, 0)),
        pl.BlockSpec((None, d, n), lambda i: (widx, 0, 0), pipeline_mode=once),
    ]
    args = [x, x, m, m, norm_g, w]
    if rope is not None:
        cos, sin = rope
        tiles_per_seq = cos.shape[0] // tm
        in_specs += [pl.BlockSpec((tm, DH_C), lambda i: (i % tiles_per_seq, 0))] * 2
        args += [cos, sin]
    out_specs = [pl.BlockSpec((tm, n), lambda i: (i, 0))]
    out_shape = [jax.ShapeDtypeStruct((t, n), out_dtype)]
    for _, n_heads, dh in head_outs:
        out_specs.append(pl.BlockSpec((None, n_heads, tm, dh), lambda i: (i, 0, 0, 0)))
        out_shape.append(jax.ShapeDtypeStruct((nt, n_heads, tm, dh), F32))
    osz = jnp.dtype(out_dtype).itemsize
    vmem = (3 * tm * d * 4 + 2 * tm * d * 2 + d * n * 2 + 2 * tm * n * osz + 4 * tm * tn * 4
            + sum(2 * n_heads * tm * dh * 4 for _, n_heads, dh in head_outs))
    return pl.pallas_call(
        functools.partial(_proj_kernel, sub=sub, tn=tn, n_q_tiles=n_q_cols // tn, q_scale=q_scale,
                          n_rope_tiles=n_rope_cols // tn, head_outs=tuple(head_outs)),
        grid=(nt,),
        in_specs=in_specs,
        out_specs=out_specs,
        out_shape=out_shape,
        scratch_shapes=[pltpu.VMEM((2, tm, d), BF16)],
        compiler_params=_params(("arbitrary",), vmem),
        name="in_proj",
    )(*args)


def _rope_tables(n):
    t = jnp.arange(n)
    nf = DH_C // 4
    inv = ROPE_BASE ** (-jnp.arange(nf, dtype=F32) / nf)
    ang_r = (t // GRID_W).astype(F32)[:, None] * inv[None, :]
    ang_c = (t % GRID_W).astype(F32)[:, None] * inv[None, :]
    cos = jnp.concatenate([jnp.cos(ang_r)] * 2 + [jnp.cos(ang_c)] * 2, axis=-1)
    sin = jnp.concatenate([-jnp.sin(ang_r), jnp.sin(ang_r), -jnp.sin(ang_c), jnp.sin(ang_c)], axis=-1)
    return cos, sin


def _outproj_kernel(*refs, ks):
    ins = refs[:len(ks)]
    w_ref, x_ref, m_ref, o_ref = refs[len(ks):]
    y = None
    off = 0
    for r, k in zip(ins, ks):
        part = _dot(r[...], w_ref[off:off + k, :])
        y = part if y is None else y + part
        off += k
    o_ref[...] = x_ref[...] + m_ref[5:6, :] * y


def _outproj(ins, w, widx, x, m, layer, cond_of_tile):
    t, d = x.shape
    ks = tuple(a.shape[1] for a in ins)
    kin = sum(ks)
    in_specs = [pl.BlockSpec((TM, k), lambda i: (i, 0)) for k in ks] + [
        pl.BlockSpec((None, kin, d), lambda i: (widx, 0, 0)),
        pl.BlockSpec((TM, d), lambda i: (i, 0)),
        pl.BlockSpec((None, None, N_MOD, d), lambda i: (layer, cond_of_tile(i), 0, 0)),
    ]
    vmem = 2 * kin * d * 2 + 2 * TM * kin * 2 + 4 * TM * d * 4 + 2 * TM * d * 4
    return pl.pallas_call(
        functools.partial(_outproj_kernel, ks=ks),
        grid=(t // TM,),
        in_specs=in_specs,
        out_specs=pl.BlockSpec((TM, d), lambda i: (i, 0)),
        out_shape=jax.ShapeDtypeStruct((t, d), F32),
        compiler_params=_params(("parallel",), vmem),
        name="out_proj",
    )(*ins, w, x, m)


def _attn_p_kernel(q_ref, k_ref, v_ref, o_ref, *, n_heads, dh):
    for h in range(n_heads):
        sl = slice(h * dh, (h + 1) * dh)
        s = _dot_nt(q_ref[:, sl].astype(BF16), k_ref[:, sl].astype(BF16))
        e = jnp.exp2(s - jnp.max(s, axis=-1, keepdims=True))
        r = 1.0 / jnp.sum(e, axis=-1, keepdims=True)
        o = _dot(e.astype(BF16), v_ref[:, sl].astype(BF16)) * r
        o_ref[:, sl] = o.astype(o_ref.dtype)


def _attn_prompt(u, n_batch, seq):
    w = H_A * DH_A
    vmem = 2 * 3 * seq * w * 4 + 2 * seq * w * 2 + 8 * seq * seq * 4
    return pl.pallas_call(
        functools.partial(_attn_p_kernel, n_heads=H_A, dh=DH_A),
        grid=(n_batch,),
        in_specs=[pl.BlockSpec((seq, w), lambda b: (b, 0)),
                  pl.BlockSpec((seq, w), lambda b: (b, 1)),
                  pl.BlockSpec((seq, w), lambda b: (b, 2))],
        out_specs=pl.BlockSpec((seq, w), lambda b: (b, 0)),
        out_shape=jax.ShapeDtypeStruct((n_batch * seq, w), BF16),
        compiler_params=_params(("parallel",), vmem),
        name="attn_prompt",
    )(u, u, u)


def _na_col_table(rpb):
    h = rpb.shape[0]
    rpb2 = rpb.astype(F32) * LOG2E
    period = GRID_W + 2 * WIN_W - 1
    rp = jnp.pad(rpb2, ((0, 0), (0, 0), (0, GRID_W)), constant_values=NEG_BIG)
    skew = jnp.tile(rp, (1, 1, GRID_W))[:, :, :GRID_W * (period - 1)].reshape(h, -1, GRID_W, period - 1)
    skew = skew[:, :, :, WIN_W - 1:WIN_W - 1 + GRID_W]
    c = np.arange(GRID_W)
    cs = np.clip(c - WIN_W // 2, 0, GRID_W - WIN_W)
    col_ok = (c[None, :] >= cs[:, None]) & (c[None, :] < cs[:, None] + WIN_W)
    tcol = jnp.where(col_ok, skew, NEG_BIG)
    return jnp.concatenate([tcol, tcol], axis=-1)


def _na_block_types(rows):
    return [(r0, min(max(r0 - (NA_KROWS - NA_QROWS) // 2, 0), rows - NA_KROWS))
            for r0 in (0, NA_QROWS, rows - NA_QROWS)]


def _na_build_bias(tc_ref, bias_scr, rows):
    kh = min(WIN_H, rows)
    lane = lax.broadcasted_iota(jnp.int32, (GRID_W, LANES), 1)
    left = lane < GRID_W
    neg = jnp.full((GRID_W, LANES), NEG_BIG, F32)
    for ty, (r0, base) in enumerate(_na_block_types(rows)):
        for rr in range(NA_QROWS):
            r = r0 + rr
            rs = min(max(r - kh // 2, 0), rows - kh)
            for p in range(NA_KROWS * GRID_W // LANES):
                halves = []
                for kr in (base + 2 * p, base + 2 * p + 1):
                    halves.append(tc_ref[kr - r + WIN_H - 1] if rs <= kr < rs + kh else None)
                if halves[0] is None and halves[1] is None:
                    tile = neg
                else:
                    tile = jnp.where(left, neg if halves[0] is None else halves[0],
                                     neg if halves[1] is None else halves[1])
                bias_scr[ty, rr * GRID_W:(rr + 1) * GRID_W, p * LANES:(p + 1) * LANES] = tile


def _na_kernel(q_ref, k_ref, v_ref, ck_ref, cv_ref, tc_ref, o_ref, ckb_scr, cvb_scr, bias_scr, ec_scr,
               el_scr, r_scr, *, rows):
    j = pl.program_id(2)
    nj = pl.num_programs(2)

    @pl.when(j == 0)
    def _():
        ckb_scr[...] = ck_ref[...].astype(BF16)
        cvb_scr[...] = cv_ref[...].astype(BF16)
        for hh in range(NA_HP):
            _na_build_bias(tc_ref.at[hh], bias_scr.at[hh], rows)

    ty = (j > 0).astype(jnp.int32) + (j == nj - 1).astype(jnp.int32)
    nk = NA_KROWS * GRID_W
    start = jnp.clip(j * NA_QROWS - (NA_KROWS - NA_QROWS) // 2, 0, rows - NA_KROWS) * GRID_W
    start = pl.multiple_of(start, 256)
    for hh in range(NA_HP):
        sl = slice(hh * DH_A, (hh + 1) * DH_A)
        bias_ref = bias_scr.at[hh, ty]
        q = q_ref[:, sl]
        s_c = _dot_nt(q, ckb_scr[hh])
        s_l = _dot_nt(q, k_ref[pl.ds(start, nk), sl])
        for rb in range(q.shape[0] // SM_RB):
            rws = slice(rb * SM_RB, (rb + 1) * SM_RB)
            sc = s_c[rws]
            sk = s_l[rws] + bias_ref[rws, :]
            mx = jnp.maximum(jnp.max(sc, axis=-1, keepdims=True), jnp.max(sk, axis=-1, keepdims=True))
            e_c = jnp.exp2(sc - mx)
            e_l = jnp.exp2(sk - mx)
            r_scr[hh, rws, :] = 1.0 / (jnp.sum(e_c, axis=-1, keepdims=True)
                                       + jnp.sum(e_l, axis=-1, keepdims=True))
            ec_scr[hh, rws, :] = e_c.astype(BF16)
            el_scr[hh, rws, :] = e_l.astype(BF16)
        o = _dot(ec_scr[hh], cvb_scr[hh]) + _dot(el_scr[hh], v_ref[pl.ds(start, nk), sl])
        o_ref[:, sl] = (o * r_scr[hh]).astype(o_ref.dtype)


def _na_latent(u, cache_k, cache_v, e, col_table, n_batch, n):
    rows = n // GRID_W
    nj = rows // NA_QROWS
    tq = NA_QROWS * GRID_W
    nk = NA_KROWS * GRID_W
    ctx = cache_k.shape[3]
    w = H_A * DH_A
    n_rel = col_table.shape[1]
    hp = NA_HP
    ng = H_A // hp
    assert 2 * GRID_W == LANES and nj >= 3 and H_A % hp == 0
    vmem = hp * (2 * tq * DH_A * 2 * 2 + 2 * 2 * n * DH_A * 2 + 2 * 2 * ctx * DH_A * 4 + 3 * tq * nk * 4
                 + 2 * n_rel * GRID_W * LANES * 4 + 2 * ctx * DH_A * 2 + tq * (nk + ctx) * 2
                 ) + 3 * tq * (nk + ctx) * 4
    return pl.pallas_call(
        functools.partial(_na_kernel, rows=rows),
        grid=(n_batch, ng, nj),
        in_specs=[
            pl.BlockSpec((tq, hp * DH_A), lambda b, h, j: (b * nj + j, h)),
            pl.BlockSpec((n, hp * DH_A), lambda b, h, j: (b, ng + h)),
            pl.BlockSpec((n, hp * DH_A), lambda b, h, j: (b, 2 * ng + h)),
            pl.BlockSpec((None, None, hp, ctx, DH_A), lambda b, h, j: (b, e, h, 0, 0)),
            pl.BlockSpec((None, None, hp, ctx, DH_A), lambda b, h, j: (b, e, h, 0, 0)),
            pl.BlockSpec((hp, n_rel, GRID_W, LANES), lambda b, h, j: (h, 0, 0, 0)),
        ],
        out_specs=pl.BlockSpec((tq, hp * DH_A), lambda b, h, j: (b * nj + j, h)),
        out_shape=jax.ShapeDtypeStruct((n_batch * n, w), BF16),
        scratch_shapes=[pltpu.VMEM((hp, ctx, DH_A), BF16), pltpu.VMEM((hp, ctx, DH_A), BF16),
                        pltpu.VMEM((hp, 3, tq, nk), F32),
                        pltpu.VMEM((hp, tq, ctx), BF16), pltpu.VMEM((hp, tq, nk), BF16),
                        pltpu.VMEM((hp, tq, 1), F32)],
        compiler_params=_params(("parallel", "parallel", "arbitrary"), vmem),
        name="na_latent",
    )(u, u, u, cache_k, cache_v, col_table)


def _conv_kernel(a_ref, g_ref, pa_ref, pg_ref, na_ref, ng_ref, w_ref, b_ref, lg_ref, lb_ref, o_ref,
                 z_scr, y_scr, zs_scr, *, tiles_per_seq):
    tm, c = o_ref.shape
    t = pl.program_id(0) % tiles_per_seq

    def glu(a, g):
        return a.astype(F32) * _sigmoid(g.astype(F32))

    z_scr[0:CONV_HALO, :] = jnp.where(t == 0, 0.0, glu(pa_ref[...], pg_ref[...]))
    z_scr[CONV_HALO:CONV_HALO + tm, :] = glu(a_ref[...], g_ref[...])
    z_scr[CONV_HALO + tm:, :] = jnp.where(t == tiles_per_seq - 1, 0.0, glu(na_ref[...], ng_ref[...]))

    first = CONV_HALO - CONV_W // 2
    win = tm + 2 * CONV_HALO - SUBLANES
    for cb in range(c // LANES):
        cols = slice(cb * LANES, (cb + 1) * LANES)
        zc = z_scr[:, cols]
        zs = zs_scr.at[cb % 2]
        for b in range(1, SUBLANES):
            zs[b] = zc[b:b + win]
        acc = jnp.zeros((tm, LANES), F32)
        for b in range(SUBLANES):
            for a8 in range(2 * CONV_HALO // SUBLANES):
                k = SUBLANES * a8 + b - first
                if 0 <= k < CONV_W:
                    rows = slice(SUBLANES * a8, SUBLANES * a8 + tm)
                    zb = z_scr[rows, cols] if b == 0 else zs[b, rows, :]
                    acc = acc + w_ref[k:k + 1, cols] * zb
        y_scr[:, cols] = acc + b_ref[:, cols]

    y = y_scr[...]
    mu = jnp.mean(y, axis=-1, keepdims=True)
    yc = y - mu
    yn = yc * lax.rsqrt(jnp.mean(yc * yc, axis=-1, keepdims=True) + EPS) * lg_ref[...] + lb_ref[...]
    o_ref[...] = (yn * _sigmoid(yn)).astype(o_ref.dtype)


def _conv_module(u, col0, c, seq, dw_w, dw_b, ln_g, ln_b):
    t = u.shape[0]
    tm = CONV_TM
    tiles_per_seq = seq // tm
    hb = tm // CONV_HALO
    n_halo = t // CONV_HALO
    ca, cg = col0 // c, col0 // c + 1
    isz = u.dtype.itemsize
    in_specs = [
        pl.BlockSpec((tm, c), lambda i: (i, ca)),
        pl.BlockSpec((tm, c), lambda i: (i, cg)),
        pl.BlockSpec((CONV_HALO, c), lambda i: (jnp.maximum(i * hb - 1, 0), ca)),
        pl.BlockSpec((CONV_HALO, c), lambda i: (jnp.maximum(i * hb - 1, 0), cg)),
        pl.BlockSpec((CONV_HALO, c), lambda i: (jnp.minimum((i + 1) * hb, n_halo - 1), ca)),
        pl.BlockSpec((CONV_HALO, c), lambda i: (jnp.minimum((i + 1) * hb, n_halo - 1), cg)),
        pl.BlockSpec((CONV_W, c), lambda i: (0, 0)),
        pl.BlockSpec((1, c), lambda i: (0, 0)),
        pl.BlockSpec((1, c), lambda i: (0, 0)),
        pl.BlockSpec((1, c), lambda i: (0, 0)),
    ]
    vmem = (4 * (tm + 2 * CONV_HALO) * c * isz + 2 * tm * c * 2 + (2 * tm + 2 * CONV_HALO) * c * 4
            + 4 * CONV_W * c * 4 + 6 * tm * c * 4)
    return pl.pallas_call(
        functools.partial(_conv_kernel, tiles_per_seq=tiles_per_seq),
        grid=(t // tm,),
        in_specs=in_specs,
        out_specs=pl.BlockSpec((tm, c), lambda i: (i, 0)),
        out_shape=jax.ShapeDtypeStruct((t, c), BF16),
        scratch_shapes=[pltpu.VMEM((tm + 2 * CONV_HALO, c), F32), pltpu.VMEM((tm, c), F32),
                        pltpu.VMEM((2, SUBLANES, tm + 2 * CONV_HALO - SUBLANES, LANES), F32)],
        compiler_params=_params(("parallel",), vmem),
        name="conv_module",
    )(u, u, u, u, u, u, dw_w, dw_b.reshape(1, c), ln_g.reshape(1, c), ln_b.reshape(1, c))


def _diff_lambda(lam_ref, lam_init):
    lp = lam_ref[...].astype(F32)
    s01 = jnp.sum(lp[0:1] * lp[1:2], axis=-1, keepdims=True)
    s23 = jnp.sum(lp[2:3] * lp[3:4], axis=-1, keepdims=True)
    return jnp.exp(s01) - jnp.exp(s23) + lam_init


def _subln(o, g_ref, lam_init):
    return o * lax.rsqrt(jnp.mean(o * o, axis=-1, keepdims=True) + EPS) * (g_ref[...] * (1.0 - lam_init))


def _diff_p_kernel(q_ref, k_ref, v_ref, lam_ref, g_ref, o_ref, *, lam_init):
    lam = _diff_lambda(lam_ref, lam_init)
    dv = 2 * DH_C
    for h in range(H_C):
        p = []
        for i in range(2):
            sl = slice((2 * h + i) * DH_C, (2 * h + i + 1) * DH_C)
            s = _dot_nt(q_ref[:, sl].astype(BF16), k_ref[:, sl].astype(BF16))
            e = jnp.exp2(s - jnp.max(s, axis=-1, keepdims=True))
            p.append(e * (1.0 / jnp.sum(e, axis=-1, keepdims=True)))
        a = (p[0] - lam * p[1]).astype(BF16)
        vs = slice(h * dv, (h + 1) * dv)
        o = _dot(a, v_ref[:, vs].astype(BF16))
        o_ref[:, vs] = _subln(o, g_ref, lam_init).astype(o_ref.dtype)


def _diff_prompt(u, lam_p, subln_g, lam_init, n_batch, seq):
    w = 2 * H_C * DH_C
    vmem = 2 * 3 * seq * w * 4 + 2 * seq * w * 2 + 10 * seq * seq * 4
    return pl.pallas_call(
        functools.partial(_diff_p_kernel, lam_init=lam_init),
        grid=(n_batch,),
        in_specs=[pl.BlockSpec((seq, w), lambda b: (b, 0)),
                  pl.BlockSpec((seq, w), lambda b: (b, 1)),
                  pl.BlockSpec((seq, w), lambda b: (b, 2)),
                  pl.BlockSpec((4, DH_C), lambda b: (0, 0)),
                  pl.BlockSpec((1, 2 * DH_C), lambda b: (0, 0))],
        out_specs=pl.BlockSpec((seq, w), lambda b: (b, 0)),
        out_shape=jax.ShapeDtypeStruct((n_batch * seq, w), BF16),
        compiler_params=_params(("parallel",), vmem),
        name="diff_prompt",
    )(u, u, u, lam_p, subln_g.reshape(1, 2 * DH_C))


def _diff_s_kernel(q_ref, k_ref, v_ref, ck_ref, cv_ref, lam_ref, g_ref, o_ref, ckb_scr, cvb_scr, e_scr,
                   acc_scr, *, lam_init):
    @pl.when(pl.program_id(2) == 0)
    def _():
        ckb_scr[...] = ck_ref[...].astype(BF16)
        cvb_scr[...] = cv_ref[...].astype(BF16)

    tq = q_ref.shape[0]
    dv = 2 * DH_C
    n_chunks = k_ref.shape[0] // DIFF_KC
    n_rb = tq // SM_RB
    lam = _diff_lambda(lam_ref, lam_init)
    for hh in range(DIFF_HP):
        vsl = slice(hh * dv, (hh + 1) * dv)
        for i in range(2):
            mi = 2 * hh + i
            sl = slice(mi * DH_C, (mi + 1) * DH_C)
            q = q_ref[:, sl]
            mx = [None] * n_rb
            den = [None] * n_rb
            for c in range(n_chunks + 1):
                if c == 0:
                    kc, vc = ckb_scr[mi], cvb_scr[hh]
                else:
                    krows = slice((c - 1) * DIFF_KC, c * DIFF_KC)
                    kc, vc = k_ref[krows, sl], v_ref[krows, vsl]
                nkc = kc.shape[0]
                s = _dot_nt(q, kc)
                alpha = [None] * n_rb
                for rb in range(n_rb):
                    rws = slice(rb * SM_RB, (rb + 1) * SM_RB)
                    sb = s[rws]
                    cmax = jnp.max(sb, axis=-1, keepdims=True)
                    if c == 0:
                        mx[rb] = cmax
                        e = jnp.exp2(sb - cmax)
                        den[rb] = jnp.sum(e, axis=-1, keepdims=True)
                    else:
                        m_new = jnp.maximum(mx[rb], cmax)
                        alpha[rb] = jnp.exp2(mx[rb] - m_new)
                        e = jnp.exp2(sb - m_new)
                        den[rb] = den[rb] * alpha[rb] + jnp.sum(e, axis=-1, keepdims=True)
                        mx[rb] = m_new
                    e_scr[hh, rws, :nkc] = e.astype(BF16)
                pv = _dot(e_scr[hh, :, :nkc], vc)
                for rb in range(n_rb):
                    rws = slice(rb * SM_RB, (rb + 1) * SM_RB)
                    if c == 0:
                        acc_scr[mi, rws, :] = pv[rws]
                    else:
                        acc_scr[mi, rws, :] = acc_scr[mi, rws, :] * alpha[rb] + pv[rws]
            for rb in range(n_rb):
                rws = slice(rb * SM_RB, (rb + 1) * SM_RB)
                acc_scr[mi, rws, :] = acc_scr[mi, rws, :] * (1.0 / den[rb])
        o = acc_scr[2 * hh] - lam * acc_scr[2 * hh + 1]
        o_ref[:, vsl] = _subln(o, g_ref, lam_init).astype(o_ref.dtype)


def _diff_latent(u, cache_k, cache_v, o_idx, lam_p, subln_g, lam_init, n_batch, n):
    tq = DIFF_TQ
    nq = n // tq
    ctx = cache_k.shape[3]
    dv = 2 * DH_C
    w = H_C * dv
    hp = DIFF_HP
    ng = H_C // hp
    assert H_C % hp == 0
    kmax = max(DIFF_KC, ctx)
    vmem = hp * (2 * tq * dv * 2 * 2 + 2 * 2 * n * dv * 2 + 2 * 2 * ctx * dv * 4 + 2 * ctx * dv * 2
                 + tq * kmax * 2 + 2 * tq * dv * 4 + 4 * tq * kmax * 4 + 2 * tq * dv * 4)
    return pl.pallas_call(
        functools.partial(_diff_s_kernel, lam_init=lam_init),
        scratch_shapes=[pltpu.VMEM((2 * hp, ctx, DH_C), BF16), pltpu.VMEM((hp, ctx, dv), BF16),
                        pltpu.VMEM((hp, tq, kmax), BF16), pltpu.VMEM((2 * hp, tq, dv), F32)],
        grid=(n_batch, ng, nq),
        in_specs=[
            pl.BlockSpec((tq, hp * dv), lambda b, h, i: (b * nq + i, h)),
            pl.BlockSpec((n, hp * dv), lambda b, h, i: (b, ng + h)),
            pl.BlockSpec((n, hp * dv), lambda b, h, i: (b, 2 * ng + h)),
            pl.BlockSpec((None, None, 2 * hp, ctx, DH_C), lambda b, h, i: (b, o_idx, h, 0, 0)),
            pl.BlockSpec((None, None, hp, ctx, dv), lambda b, h, i: (b, o_idx, h, 0, 0)),
            pl.BlockSpec((4, DH_C), lambda b, h, i: (0, 0)),
            pl.BlockSpec((1, dv), lambda b, h, i: (0, 0)),
        ],
        out_specs=pl.BlockSpec((tq, hp * dv), lambda b, h, i: (b * nq + i, h)),
        out_shape=jax.ShapeDtypeStruct((n_batch * n, w), BF16),
        compiler_params=_params(("parallel", "parallel", "arbitrary"), vmem),
        name="diff_latent",
    )(u, u, u, cache_k, cache_v, lam_p, subln_g.reshape(1, dv))


def kernel(x_prompt, x_sample, c, cache_a_k, cache_a_v, cache_c_k, cache_c_v, c_ctx, w_mod, b_mod, norm_g,
           ffn_w1, ffn_w3, ffn_w2, a_w_in, a_w_out, a_rpb, b_dw_w, b_dw_b, b_ln_g, b_ln_b, c_w_in, c_w_out,
           c_lambda, c_subln_g, final_g):
    bp, seq, d = x_prompt.shape
    bs, n, _ = x_sample.shape
    depth = w_mod.shape[0]
    assert 1 + bs <= N_COND and n % TM_FFN == 0 and (bp * seq) % TM_FFN == 0 and TM_FFN % TM == 0

    xp = x_prompt.reshape(bp * seq, d)
    xs = x_sample.reshape(bs * n, d)
    conds = jnp.concatenate([c_ctx[None], c, jnp.zeros((N_COND - 1 - bs, d), F32)], axis=0)
    m = _modulation(conds, w_mod, b_mod)

    tiles_per_latent = n // TM
    cond_p = lambda i: 0
    cond_s = lambda i: 1 + i // tiles_per_latent

    w1 = ffn_w1.astype(BF16)
    w3 = ffn_w3.astype(BF16)
    w2 = ffn_w2.astype(BF16)
    a_in, a_out = a_w_in.astype(BF16), a_w_out.astype(BF16)
    c_in, c_out = c_w_in.astype(BF16), c_w_out.astype(BF16)
    ng = norm_g.reshape(depth, N_SUB, 1, d)
    rope = _rope_tables(n)

    new_a_k, new_a_v, new_c_k, new_c_v = [], [], [], []
    for l in range(depth):
        xp = _ffn(xp, m, ng, l, cond_p, w1, w3, w2, 0, 0)
        xs = _ffn(xs, m, ng, l, cond_s, w1, w3, w2, 0, 0)
        if l % 2 == 0:
            e = l // 2
            wa = H_A * DH_A
            cb = b_dw_w.shape[2]
            conv_p = (b_dw_w[e], b_dw_b[e], b_ln_g[e], b_ln_b[e])
            qs = DH_A ** -0.5 * LOG2E
            up, k_heads, v_heads = _proj(xp, m, ng, l, cond_p, a_in, e, 1, F32, seq, wa, qs,
                                         head_outs=[(wa, H_A, DH_A), (2 * wa, H_A, DH_A)])
            us, = _proj(xs, m, ng, l, cond_s, a_in, e, 1, BF16, TM, wa, qs)
            new_a_k.append(k_heads)
            new_a_v.append(v_heads)
            op = _attn_prompt(up, bp, seq)
            cp = _conv_module(up, 3 * wa, cb, seq, *conv_p)
            os_ = _na_latent(us, cache_a_k, cache_a_v, e, _na_col_table(a_rpb[e]), bs, n)
            cs = _conv_module(us, 3 * wa, cb, n, *conv_p)
            xp = _outproj([op, cp], a_out, e, xp, m, l, cond_p)
            xs = _outproj([os_, cs], a_out, e, xs, m, l, cond_s)
        else:
            o = l // 2
            lam_init = 0.8 - 0.6 * math.exp(-0.3 * l)
            wqk = 2 * H_C * DH_C
            qs = DH_C ** -0.5 * LOG2E
            up, k_heads, v_heads = _proj(xp, m, ng, l, cond_p, c_in, o, 1, F32, seq, wqk, qs,
                                         head_outs=[(wqk, 2 * H_C, DH_C), (2 * wqk, H_C, 2 * DH_C)])
            us, = _proj(xs, m, ng, l, cond_s, c_in, o, 1, BF16, TM, wqk, qs, rope=rope,
                        n_rope_cols=2 * wqk)
            new_c_k.append(k_heads)
            new_c_v.append(v_heads)
            op = _diff_prompt(up, c_lambda[o], c_subln_g[o], lam_init, bp, seq)
            os_ = _diff_latent(us, cache_c_k, cache_c_v, o, c_lambda[o], c_subln_g[o], lam_init, bs, n)
            xp = _outproj([op], c_out, o, xp, m, l, cond_p)
            xs = _outproj([os_], c_out, o, xs, m, l, cond_s)
        fg = final_g if l == depth - 1 else None
        xp = _ffn(xp, m, ng, l, cond_p, w1, w3, w2, 1, 2, final_g=fg)
        xs = _ffn(xs, m, ng, l, cond_s, w1, w3, w2, 1, 2, final_g=fg)

    return (xp.reshape(bp, seq, d), xs.reshape(bs, n, d),
            jnp.stack(new_a_k, axis=1), jnp.stack(new_a_v, axis=1),
            jnp.stack(new_c_k, axis=1), jnp.stack(new_c_v, axis=1))
```
